```python
import math
import jax, jax.numpy as jnp
from jax import lax
import numpy as np


D_MODEL = 2048
BATCH = 1
SEQ = 16384
DEPTH = 2

CHUNK = 64
QBLOCK = 128
N_MIXERS = 2
CONV_KERNEL = 31
HEAD_DIM = 64
N_HEADS = D_MODEL // (2 * HEAD_DIM)
D_FF = ((8 * D_MODEL // 3 + 255) // 256) * 256
FFN_KERNEL = 3
NUM_BUCKETS = 32
MAX_DISTANCE = 128
EPS = 1e-6
N_CONV_LAYERS = (DEPTH + 1) // 2
N_ATTN_LAYERS = DEPTH // 2

kernel_name = 'hybrid_conformer_diffattn_convffn_encoder'


def _rmsnorm(x, g):
    xf = x.astype(jnp.float32)
    y = xf * lax.rsqrt(jnp.mean(xf * xf, axis=-1, keepdims=True) + EPS)
    return (y * g.astype(jnp.float32)).astype(x.dtype)


def _layernorm(x, g, b):
    xf = x.astype(jnp.float32)
    mu = jnp.mean(xf, axis=-1, keepdims=True)
    var = jnp.mean(jnp.square(xf - mu), axis=-1, keepdims=True)
    y = (xf - mu) * lax.rsqrt(var + EPS)
    return (y * g.astype(jnp.float32) + b.astype(jnp.float32)).astype(x.dtype)


def _causal_dwconv(x, w, b):
    K, C = w.shape
    y = lax.conv_general_dilated(
        x, w[:, None, :].astype(x.dtype), window_strides=(1,), padding=[(K - 1, 0)],
        dimension_numbers=('NWC', 'WIO', 'NWC'), feature_group_count=C)
    return y + b.astype(x.dtype)


def _t5_bucket(rel):
    half = NUM_BUCKETS // 2
    max_exact = half // 2
    ret = jnp.where(rel > 0, half, 0)
    n = jnp.abs(rel)
    nf = jnp.maximum(n, 1).astype(jnp.float32)
    large = max_exact + (jnp.log(nf / max_exact) / math.log(MAX_DISTANCE / max_exact)
                         * (half - max_exact)).astype(jnp.int32)
    large = jnp.minimum(large, half - 1)
    return ret + jnp.where(n < max_exact, n, large)


def _conformer_conv(h, w_in, b_in, dw_w, dw_b, ln_g, ln_b, w_out, b_out):
    u = h @ w_in + b_in
    a, gate = jnp.split(u, 2, axis=-1)
    u = a * jax.nn.sigmoid(gate)
    u = _causal_dwconv(u, dw_w, dw_b)
    u = jax.nn.silu(_layernorm(u, ln_g, ln_b))
    return u @ w_out + b_out


def _diff_attention(h, w_qkv, lq1, lk1, lq2, lk2, subln_g, w_o, rel_bias, lambda_init):
    B, S, _ = h.shape
    qkv = h @ w_qkv
    q, k, v = jnp.split(qkv, 3, axis=-1)
    q = q.reshape(B, S, N_HEADS, 2, HEAD_DIM) * (HEAD_DIM ** -0.5)
    k = k.reshape(B, S, N_HEADS, 2, HEAD_DIM)
    v = v.reshape(B, S, N_HEADS, 2 * HEAD_DIM)
    lam = (jnp.exp(jnp.sum(lq1.astype(jnp.float32) * lk1.astype(jnp.float32)))
           - jnp.exp(jnp.sum(lq2.astype(jnp.float32) * lk2.astype(jnp.float32)))
           + lambda_init)
    nblk = S // QBLOCK
    q_blocks = q.reshape(B, nblk, QBLOCK, N_HEADS, 2, HEAD_DIM).transpose(1, 0, 2, 3, 4, 5)
    k_pos = jnp.arange(S, dtype=jnp.int32)
    k_chunk = k_pos // CHUNK
    bias_table = rel_bias.T.astype(jnp.float32)

    def one_block(args):
        qb, bi = args
        q_pos = bi * QBLOCK + jnp.arange(QBLOCK, dtype=jnp.int32)
        s = jnp.einsum('bqhcd,bkhcd->bhcqk', qb, k).astype(jnp.float32)
        bucket = _t5_bucket(k_pos[None, :] - q_pos[:, None])
        bias = bias_table[:, bucket]
        allowed = k_chunk[None, :] <= (q_pos // CHUNK)[:, None]
        s = jnp.where(allowed, s + bias[None, :, None], -jnp.inf)
        p = jax.nn.softmax(s, axis=-1)
        a = p[:, :, 0] - lam * p[:, :, 1]
        return jnp.einsum('bhqk,bkhe->bqhe', a.astype(v.dtype), v)

    o = lax.map(one_block, (q_blocks, jnp.arange(nblk, dtype=jnp.int32)))
    o = o.transpose(1, 0, 2, 3, 4).reshape(B, S, N_HEADS, 2 * HEAD_DIM)
    o = _rmsnorm(o, subln_g) * (1.0 - lambda_init)
    return o.reshape(B, S, D_MODEL) @ w_o


def _conv_ffn(h, w_up, dw_w, dw_b, w_down):
    u = h @ w_up
    u = _causal_dwconv(u, dw_w, dw_b)
    gate, val = jnp.split(u, 2, axis=-1)
    return (jax.nn.silu(gate) * val) @ w_down


def setup_inputs(seed: int = 0) -> dict:
    key = jax.random.key(seed)
    ks = jax.random.split(key, 24)
    D, F = D_MODEL, D_FF
    nc, na = N_CONV_LAYERS, N_ATTN_LAYERS
    nrm = lambda k, shape, scale: jax.random.normal(k, shape, jnp.float32) * scale
    return {
        'x': nrm(ks[0], (BATCH, SEQ, D), 1.0),
        'mix_norm': 1.0 + nrm(ks[1], (DEPTH, D), 0.02),
        'ffn_norm': 1.0 + nrm(ks[2], (DEPTH, D), 0.02),
        'conv_w_in': nrm(ks[3], (nc, D, 2 * D), D ** -0.5),
        'conv_b_in': nrm(ks[4], (nc, 2 * D), 0.02),
        'conv_dw_w': nrm(ks[5], (nc, CONV_KERNEL, D), CONV_KERNEL ** -0.5),
        'conv_dw_b': nrm(ks[6], (nc, D), 0.02),
        'conv_ln_g': 1.0 + nrm(ks[7], (nc, D), 0.02),
        'conv_ln_b': nrm(ks[8], (nc, D), 0.02),
        'conv_w_out': nrm(ks[9], (nc, D, D), D ** -0.5),
        'conv_b_out': nrm(ks[10], (nc, D), 0.02),
        'attn_w_qkv': nrm(ks[11], (na, D, 3 * D), D ** -0.5),
        'attn_lambda_q1': nrm(ks[12], (na, HEAD_DIM), 0.1),
        'attn_lambda_k1': nrm(ks[13], (na, HEAD_DIM), 0.1),
        'attn_lambda_q2': nrm(ks[14], (na, HEAD_DIM), 0.1),
        'attn_lambda_k2': nrm(ks[15], (na, HEAD_DIM), 0.1),
        'attn_subln_g': 1.0 + nrm(ks[16], (na, 2 * HEAD_DIM), 0.02),
        'attn_w_o': nrm(ks[17], (na, D, D), D ** -0.5),
        'rel_bias': nrm(ks[18], (NUM_BUCKETS, N_HEADS), 0.2),
        'ffn_w_up': nrm(ks[19], (DEPTH, D, 2 * F), D ** -0.5),
        'ffn_dw_w': nrm(ks[20], (DEPTH, FFN_KERNEL, 2 * F), FFN_KERNEL ** -0.5),
        'ffn_dw_b': nrm(ks[21], (DEPTH, 2 * F), 0.02),
        'ffn_w_down': nrm(ks[22], (DEPTH, F, D), F ** -0.5),
        'final_norm_g': 1.0 + nrm(ks[23], (D,), 0.02),
    }


def reference(x, mix_norm, ffn_norm, conv_w_in, conv_b_in, conv_dw_w, conv_dw_b, conv_ln_g,
              conv_ln_b, conv_w_out, conv_b_out, attn_w_qkv, attn_lambda_q1, attn_lambda_k1,
              attn_lambda_q2, attn_lambda_k2, attn_subln_g, attn_w_o, rel_bias, ffn_w_up,
              ffn_dw_w, ffn_dw_b, ffn_w_down, final_norm_g):
    for i in range(DEPTH):
        j = i // N_MIXERS
        h = _rmsnorm(x, mix_norm[i])
        if i % N_MIXERS == 0:
            m = _conformer_conv(h, conv_w_in[j], conv_b_in[j], conv_dw_w[j], conv_dw_b[j],
                                conv_ln_g[j], conv_ln_b[j], conv_w_out[j], conv_b_out[j])
        else:
            lambda_init = 0.8 - 0.6 * math.exp(-0.3 * i)
            m = _diff_attention(h, attn_w_qkv[j], attn_lambda_q1[j], attn_lambda_k1[j],
                                attn_lambda_q2[j], attn_lambda_k2[j], attn_subln_g[j],
                                attn_w_o[j], rel_bias, lambda_init)
        x = x + m
        h = _rmsnorm(x, ffn_norm[i])
        x = x + _conv_ffn(h, ffn_w_up[i], ffn_dw_w[i], ffn_dw_b[i], ffn_w_down[i])
    return _rmsnorm(x, final_norm_g)
```

```python
import functools
import math

import numpy as np
import jax
import jax.numpy as jnp
from jax import lax
from jax.experimental import pallas as pl
from jax.experimental.pallas import tpu as pltpu

F32 = jnp.float32
BF16 = jnp.bfloat16

CHUNK = 64
HEAD_DIM = 64
NUM_BUCKETS = 32
MAX_DISTANCE = 128
EPS = 1e-6
N_MIXERS = 2

V7X_VMEM_BYTES = 64 * 1024 * 1024
V7X_F32_SUBLANES = 8
V7X_LANES = 128
VMEM_TEMP_BYTES = 12 * 1024 * 1024

ATTN_BLOCK = 512
CONV_ROWS = 128
CONV_HALO = 32
FFN_CARRY = V7X_F32_SUBLANES


def _nbytes(shape, dtype):
    return int(np.prod(shape)) * jnp.dtype(dtype).itemsize


def _params(semantics, blocks, scratch=()):
    need = 2 * sum(_nbytes(s, d) for s, d in blocks) + sum(_nbytes(s, d) for s, d in scratch)
    limit = min(need + VMEM_TEMP_BYTES, V7X_VMEM_BYTES - 4 * 1024 * 1024)
    return pltpu.CompilerParams(dimension_semantics=semantics, vmem_limit_bytes=limit)


def _rmsnorm_kernel(x_ref, g_ref, o_ref):
    x = x_ref[...]
    ms = jnp.mean(x * x, axis=-1, keepdims=True)
    o_ref[...] = (x * lax.rsqrt(ms + EPS) * g_ref[...]).astype(o_ref.dtype)


def _rmsnorm(x, g, out_dtype, bm=256):
    s, d = x.shape
    return pl.pallas_call(
        _rmsnorm_kernel,
        grid=(s // bm,),
        in_specs=[pl.BlockSpec((bm, d), lambda i: (i, 0)), pl.BlockSpec((1, d), lambda i: (0, 0))],
        out_specs=pl.BlockSpec((bm, d), lambda i: (i, 0)),
        out_shape=jax.ShapeDtypeStruct((s, d), out_dtype),
        compiler_params=_params(("parallel",), [((bm, d), F32), ((bm, d), out_dtype)]),
        name="rmsnorm",
    )(x, g.reshape(1, d))


def _mm_glu_kernel(h_ref, wa_ref, wg_ref, ba_ref, bg_ref, o_ref):
    h = h_ref[...]
    a = jnp.dot(h, wa_ref[...], preferred_element_type=F32) + ba_ref[...]
    g = jnp.dot(h, wg_ref[...], preferred_element_type=F32) + bg_ref[...]
    o_ref[...] = a * jax.nn.sigmoid(g)


def _mm_glu(h, w, b, bm=1024, bn=512):
    s, k = h.shape
    n = w.shape[1] // 2
    nb = n // bn
    b = b.reshape(1, 2 * n)
    return pl.pallas_call(
        _mm_glu_kernel,
        grid=(s // bm, nb),
        in_specs=[
            pl.BlockSpec((bm, k), lambda i, j: (i, 0)),
            pl.BlockSpec((k, bn), lambda i, j: (0, j)),
            pl.BlockSpec((k, bn), lambda i, j: (0, j + nb)),
            pl.BlockSpec((1, bn), lambda i, j: (0, j)),
            pl.BlockSpec((1, bn), lambda i, j: (0, j + nb)),
        ],
        out_specs=pl.BlockSpec((bm, bn), lambda i, j: (i, j)),
        out_shape=jax.ShapeDtypeStruct((s, n), F32),
        compiler_params=_params(
            ("parallel", "parallel"),
            [((bm, k), BF16), ((k, bn), BF16), ((k, bn), BF16), ((bm, bn), F32)]),
        name="mm_glu",
    )(h, w, w, b, b)


def _mm_res_kernel(a_ref, w_ref, r_ref, o_ref):
    o_ref[...] = r_ref[...] + jnp.dot(a_ref[...], w_ref[...], preferred_element_type=F32)


def _mm_res_bias_kernel(a_ref, w_ref, b_ref, r_ref, o_ref):
    o_ref[...] = r_ref[...] + (jnp.dot(a_ref[...], w_ref[...], preferred_element_type=F32) + b_ref[...])


def _mm_res(a, w, res, bias=None, bm=1024, bn=512):
    s, k = a.shape
    n = w.shape[1]
    specs = [pl.BlockSpec((bm, k), lambda i, j: (i, 0)), pl.BlockSpec((k, bn), lambda i, j: (0, j))]
    args = [a, w]
    if bias is not None:
        specs.append(pl.BlockSpec((1, bn), lambda i, j: (0, j)))
        args.append(bias.reshape(1, n))
    specs.append(pl.BlockSpec((bm, bn), lambda i, j: (i, j)))
    args.append(res)
    return pl.pallas_call(
        _mm_res_kernel if bias is None else _mm_res_bias_kernel,
        grid=(s // bm, n // bn),
        in_specs=specs,
        out_specs=pl.BlockSpec((bm, bn), lambda i, j: (i, j)),
        out_shape=jax.ShapeDtypeStruct((s, n), F32),
        compiler_params=_params(
            ("parallel", "parallel"),
            [((bm, k), BF16), ((k, bn), BF16), ((bm, bn), F32), ((bm, bn), F32)]),
        name="mm_res",
    )(*args)


def _mm_qk_kernel(h_ref, w_ref, o_ref, *, n_scaled_blocks, scale):
    acc = jnp.dot(h_ref[...], w_ref[...], preferred_element_type=F32)
    mult = jnp.where(pl.program_id(1) < n_scaled_blocks, scale, 1.0).astype(F32)
    o_ref[...] = (acc * mult).astype(o_ref.dtype)


def _mm_qk(h, w, n_scaled, scale, bm=1024, bn=1024):
    s, k = h.shape
    n = w.shape[1]
    return pl.pallas_call(
        functools.partial(_mm_qk_kernel, n_scaled_blocks=n_scaled // bn, scale=scale),
        grid=(s // bm, n // bn),
        in_specs=[pl.BlockSpec((bm, k), lambda i, j: (i, 0)), pl.BlockSpec((k, bn), lambda i, j: (0, j))],
        out_specs=pl.BlockSpec((bm, bn), lambda i, j: (i, j)),
        out_shape=jax.ShapeDtypeStruct((s, n), BF16),
        compiler_params=_params(
            ("parallel", "parallel"), [((bm, k), BF16), ((k, bn), BF16), ((bm, bn), BF16)],
            scratch=[((bm, bn), F32)]),
        name="mm_qk",
    )(h, w)


def _mm_nt_kernel(wt_ref, h_ref, o_ref):
    o_ref[0] = lax.dot_general(
        wt_ref[...], h_ref[...], (((1,), (1,)), ((), ())), preferred_element_type=F32).astype(o_ref.dtype)


def _mm_nt(h, wt, bm, bn=1024):
    s, k = h.shape
    n = wt.shape[0]
    return pl.pallas_call(
        _mm_nt_kernel,
        grid=(s // bm, n // bn),
        in_specs=[pl.BlockSpec((bn, k), lambda i, j: (j, 0)), pl.BlockSpec((bm, k), lambda i, j: (i, 0))],
        out_specs=pl.BlockSpec((1, bn, bm), lambda i, j: (i, j, 0)),
        out_shape=jax.ShapeDtypeStruct((s // bm, n, bm), BF16),
        compiler_params=_params(
            ("parallel", "parallel"), [((bn, k), BF16), ((bm, k), BF16), ((bn, bm), BF16)],
            scratch=[((bn, bm), F32)]),
        name="mm_nt",
    )(wt, h)


def _conv_ln_kernel(cur_ref, halo_ref, w_ref, b_ref, g_ref, beta_ref, o_ref, buf_ref, y_ref, *, taps):
    rows, d = cur_ref.shape
    halo = halo_ref[...]
    buf_ref[0:CONV_HALO, :] = jnp.where(pl.program_id(0) == 0, jnp.zeros_like(halo), halo)
    buf_ref[CONV_HALO:CONV_HALO + rows, :] = cur_ref[...]
    first = CONV_HALO - (taps - 1)
    for c in range(d // V7X_LANES):
        cs = slice(c * V7X_LANES, (c + 1) * V7X_LANES)
        acc = jnp.broadcast_to(b_ref[:, cs], (rows, V7X_LANES))
        for t in range(taps):
            acc = acc + w_ref[t:t + 1, cs] * buf_ref[first + t:first + t + rows, cs]
        y_ref[:, cs] = acc
    y = y_ref[...]
    mu = jnp.mean(y, axis=-1, keepdims=True)
    yc = y - mu
    var = jnp.mean(yc * yc, axis=-1, keepdims=True)
    z = yc * lax.rsqrt(var + EPS) * g_ref[...] + beta_ref[...]
    o_ref[...] = (z * jax.nn.sigmoid(z)).astype(o_ref.dtype)


def _conv_ln(u, dw_w, dw_b, ln_g, ln_b):
    s, d = u.shape
    taps = dw_w.shape[0]
    rows = CONV_ROWS
    per = rows // CONV_HALO
    vec = lambda v: v.reshape(1, d)
    return pl.pallas_call(
        functools.partial(_conv_ln_kernel, taps=taps),
        grid=(s // rows,),
        in_specs=[
            pl.BlockSpec((rows, d), lambda i: (i, 0)),
            pl.BlockSpec((CONV_HALO, d), lambda i: (jnp.maximum(i * per - 1, 0), 0)),
            pl.BlockSpec((taps, d), lambda i: (0, 0)),
            pl.BlockSpec((1, d), lambda i: (0, 0)),
            pl.BlockSpec((1, d), lambda i: (0, 0)),
            pl.BlockSpec((1, d), lambda i: (0, 0)),
        ],
        out_specs=pl.BlockSpec((rows, d), lambda i: (i, 0)),
        out_shape=jax.ShapeDtypeStruct((s, d), BF16),
        scratch_shapes=[pltpu.VMEM((rows + CONV_HALO, d), F32), pltpu.VMEM((rows, d), F32)],
        compiler_params=_params(
            ("parallel",), [((rows, d), F32), ((CONV_HALO, d), F32), ((taps, d), F32), ((rows, d), BF16)],
            scratch=[((rows + CONV_HALO, d), F32), ((rows, d), F32)]),
        name="conv_ln",
    )(u, u, dw_w, vec(dw_b), vec(ln_g), vec(ln_b))


def _ffn_up_kernel(h_ref, wg_ref, wv_ref, cwg_ref, cwv_ref, cbg_ref, cbv_ref, o_ref, bufg_ref, bufv_ref):
    rows = h_ref.shape[0]
    c = FFN_CARRY

    @pl.when(pl.program_id(1) == 0)
    def _():
        bufg_ref[0:c, :] = jnp.zeros((c, bufg_ref.shape[1]), F32)
        bufv_ref[0:c, :] = jnp.zeros((c, bufv_ref.shape[1]), F32)

    h = h_ref[...]
    bufg_ref[c:c + rows, :] = jnp.dot(h, wg_ref[...], preferred_element_type=F32)
    bufv_ref[c:c + rows, :] = jnp.dot(h, wv_ref[...], preferred_element_type=F32)

    def conv(buf_ref, cw_ref, cb_ref):
        return (cw_ref[0:1, :] * buf_ref[c - 2:c - 2 + rows, :]
                + cw_ref[1:2, :] * buf_ref[c - 1:c - 1 + rows, :]
                + cw_ref[2:3, :] * buf_ref[c:c + rows, :]
                + cb_ref[...])

    gate = conv(bufg_ref, cwg_ref, cbg_ref)
    val = conv(bufv_ref, cwv_ref, cbv_ref)
    o_ref[...] = (gate * jax.nn.sigmoid(gate) * val).astype(o_ref.dtype)
    bufg_ref[0:c, :] = bufg_ref[rows:rows + c, :]
    bufv_ref[0:c, :] = bufv_ref[rows:rows + c, :]


def _ffn_up(h, w_up, dw_w, dw_b, bm=1024, bn=512):
    s, k = h.shape
    f = w_up.shape[1] // 2
    nb = f // bn
    taps = dw_w.shape[0]
    assert taps - 1 <= FFN_CARRY
    dw_b = dw_b.reshape(1, 2 * f)
    return pl.pallas_call(
        _ffn_up_kernel,
        grid=(nb, s // bm),
        in_specs=[
            pl.BlockSpec((bm, k), lambda j, i: (i, 0)),
            pl.BlockSpec((k, bn), lambda j, i: (0, j)),
            pl.BlockSpec((k, bn), lambda j, i: (0, j + nb)),
            pl.BlockSpec((taps, bn), lambda j, i: (0, j)),
            pl.BlockSpec((taps, bn), lambda j, i: (0, j + nb)),
            pl.BlockSpec((1, bn), lambda j, i: (0, j)),
            pl.BlockSpec((1, bn), lambda j, i: (0, j + nb)),
        ],
        out_specs=pl.BlockSpec((bm, bn), lambda j, i: (i, j)),
        out_shape=jax.ShapeDtypeStruct((s, f), BF16),
        scratch_shapes=[pltpu.VMEM((bm + FFN_CARRY, bn), F32), pltpu.VMEM((bm + FFN_CARRY, bn), F32)],
        compiler_params=_params(
            ("arbitrary", "arbitrary"),
            [((bm, k), BF16), ((k, bn), BF16), ((k, bn), BF16), ((bm, bn), BF16)],
            scratch=[((bm + FFN_CARRY, bn), F32), ((bm + FFN_CARRY, bn), F32)]),
        name="ffn_up",
    )(h, w_up, w_up, dw_w, dw_w, dw_b, dw_b)


def _t5_bucket(rel):
    half = NUM_BUCKETS // 2
    max_exact = half // 2
    ret = jnp.where(rel > 0, half, 0)
    n = jnp.abs(rel)
    nf = jnp.maximum(n, 1).astype(jnp.float32)
    large = max_exact + (jnp.log(nf / max_exact) / math.log(MAX_DISTANCE / max_exact)
                         * (half - max_exact)).astype(jnp.int32)
    large = jnp.minimum(large, half - 1)
    return ret + jnp.where(n < max_exact, n, large)


def _far_bucket(block, seq):
    n = np.arange(block + 1, seq + 1, dtype=np.float64)
    half = NUM_BUCKETS // 2
    max_exact = half // 2
    large = max_exact + (np.log(n / max_exact) / math.log(MAX_DISTANCE / max_exact) * (half - max_exact)).astype(np.int64)
    assert large.min() > half - 1, "attention block too small for a constant far-key bias"
    return half - 1


def _bias_tiles_kernel(tbl_ref, bucket_ref, allowed_ref, o_ref, *, far_bucket):
    h = pl.program_id(0)
    far = tbl_ref[h, far_bucket]
    for t in range(2):
        bucket = bucket_ref[t]
        acc = jnp.zeros(bucket.shape, F32)
        for b in range(NUM_BUCKETS):
            acc = jnp.where(bucket == b, tbl_ref[h, b] - far, acc)
        o_ref[0, t] = jnp.where(allowed_ref[t] != 0, acc, -jnp.inf)


def _bias_tiles(rel_bias, block, seq):
    n_heads = rel_bias.shape[1]
    key = jnp.arange(block, dtype=jnp.int32)[:, None]
    qry = jnp.arange(block, dtype=jnp.int32)[None, :]
    rel = jnp.stack([key - qry, key - qry - block])
    bucket = _t5_bucket(rel)
    allowed = jnp.stack([(key // CHUNK) <= (qry // CHUNK), jnp.ones((block, block), bool)]).astype(jnp.int32)
    return pl.pallas_call(
        functools.partial(_bias_tiles_kernel, far_bucket=_far_bucket(block, seq)),
        grid=(n_heads,),
        in_specs=[
            pl.BlockSpec(memory_space=pltpu.SMEM),
            pl.BlockSpec((2, block, block), lambda h: (0, 0, 0)),
            pl.BlockSpec((2, block, block), lambda h: (0, 0, 0)),
        ],
        out_specs=pl.BlockSpec((1, 2, block, block), lambda h: (h, 0, 0, 0)),
        out_shape=jax.ShapeDtypeStruct((n_heads, 2, block, block), F32),
        compiler_params=_params(
            ("parallel",),
            [((2, block, block), jnp.int32), ((2, block, block), jnp.int32), ((2, block, block), F32)]),
        name="bias_tiles",
    )(rel_bias.T, bucket, allowed)


def _attn_kernel(q_ref, k_ref, vt_ref, bias_ref, lam_ref, g_ref, o_ref, m_ref, l_ref, acc_ref, *, lambda_init):
    blk = q_ref.shape[0]
    qi = pl.program_id(1)
    q = q_ref[...]
    lane = lax.broadcasted_iota(jnp.int32, q.shape, 1)
    zero = jnp.zeros_like(q)
    q_parts = (jnp.where(lane < HEAD_DIM, q, zero), jnp.where(lane >= HEAD_DIM, q, zero))

    m_ref[...] = jnp.full(m_ref.shape, -jnp.inf, F32)
    l_ref[...] = jnp.zeros(l_ref.shape, F32)
    acc_ref[...] = jnp.zeros(acc_ref.shape, F32)

    def step(kb, bias):
        k = k_ref[pl.ds(pl.multiple_of(kb * blk, blk), blk), :]
        vt = vt_ref[kb]
        for c in range(2):
            s = lax.dot_general(k, q_parts[c], (((1,), (1,)), ((), ())), preferred_element_type=F32)
            if bias is not None:
                s = s + bias
            m_old = m_ref[c]
            m_new = jnp.maximum(m_old, jnp.max(s, axis=0, keepdims=True))
            alpha = jnp.exp(m_old - m_new)
            p = jnp.exp(s - m_new)
            l_ref[c] = alpha * l_ref[c] + jnp.sum(p, axis=0, keepdims=True)
            acc_ref[c] = alpha * acc_ref[c] + jnp.dot(vt, p.astype(vt.dtype), preferred_element_type=F32)
            m_ref[c] = m_new

    def far_step(kb, carry):
        step(kb, None)
        return carry

    lax.fori_loop(0, qi - 1, far_step, 0)

    @pl.when(qi >= 1)
    def _():
        step(qi - 1, bias_ref[0, 1])

    step(qi, bias_ref[0, 0])

    lam_v = lam_ref[...]
    lam = (jnp.exp(jnp.sum(lam_v[0:1] * lam_v[1:2], axis=-1, keepdims=True))
           - jnp.exp(jnp.sum(lam_v[2:3] * lam_v[3:4], axis=-1, keepdims=True)) + lambda_init)
    o = acc_ref[0] / l_ref[0] - lam * (acc_ref[1] / l_ref[1])
    ms = jnp.mean(o * o, axis=0, keepdims=True)
    y = o * lax.rsqrt(ms + EPS) * g_ref[...] * (1.0 - lambda_init)
    o_ref[...] = y.T.astype(o_ref.dtype)


def _attention(qk, vt, bias, lam_vecs, subln_g, lambda_init):
    s = qk.shape[0]
    nb, d, blk = vt.shape
    hd = 2 * HEAD_DIM
    n_heads = d // hd
    return pl.pallas_call(
        functools.partial(_attn_kernel, lambda_init=lambda_init),
        grid=(n_heads, nb),
        in_specs=[
            pl.BlockSpec((blk, hd), lambda h, i: (i, h)),
            pl.BlockSpec((s, hd), lambda h, i: (0, n_heads + h)),
            pl.BlockSpec((nb, hd, blk), lambda h, i: (0, h, 0)),
            pl.BlockSpec((1, 2, blk, blk), lambda h, i: (h, 0, 0, 0)),
            pl.BlockSpec((4, HEAD_DIM), lambda h, i: (0, 0)),
            pl.BlockSpec((hd, 1), lambda h, i: (0, 0)),
        ],
        out_specs=pl.BlockSpec((blk, hd), lambda h, i: (i, h)),
        out_shape=jax.ShapeDtypeStruct((s, d), BF16),
        scratch_shapes=[
            pltpu.VMEM((2, 1, blk), F32), pltpu.VMEM((2, 1, blk), F32), pltpu.VMEM((2, hd, blk), F32)],
        compiler_params=_params(
            ("parallel", "parallel"),
            [((blk, hd), BF16), ((s, hd), BF16), ((nb, hd, blk), BF16), ((2, blk, blk), F32), ((blk, hd), BF16)],
            scratch=[((2, hd, blk), F32), ((6, blk, blk), F32)]),
        name="diff_attention",
    )(qk, qk, vt, bias, lam_vecs, subln_g.reshape(hd, 1))


def _conv_ffn(x, norm_g, w_up, dw_w, dw_b, w_down):
    h = _rmsnorm(x, norm_g, BF16)
    act = _ffn_up(h, w_up.astype(BF16), dw_w, dw_b)
    return _mm_res(act, w_down.astype(BF16), x)


def kernel(x, mix_norm, ffn_norm, conv_w_in, conv_b_in, conv_dw_w, conv_dw_b, conv_ln_g, conv_ln_b, conv_w_out, conv_b_out, attn_w_qkv, attn_lambda_q1, attn_lambda_k1, attn_lambda_q2, attn_lambda_k2, attn_subln_g, attn_w_o, rel_bias, ffn_w_up, ffn_dw_w, ffn_dw_b, ffn_w_down, final_norm_g):
    batch, seq, d = x.shape
    assert batch == 1
    depth = mix_norm.shape[0]
    x = x.reshape(seq, d)
    for i in range(depth):
        j = i // N_MIXERS
        h = _rmsnorm(x, mix_norm[i], BF16)
        if i % N_MIXERS == 0:
            u = _mm_glu(h, conv_w_in[j].astype(BF16), conv_b_in[j])
            c = _conv_ln(u, conv_dw_w[j], conv_dw_b[j], conv_ln_g[j], conv_ln_b[j])
            x = _mm_res(c, conv_w_out[j].astype(BF16), x, bias=conv_b_out[j])
        else:
            lambda_init = 0.8 - 0.6 * math.exp(-0.3 * i)
            w_qkv = attn_w_qkv[j].astype(BF16)
            qk = _mm_qk(h, w_qkv[:, :2 * d], n_scaled=d, scale=HEAD_DIM ** -0.5)
            vt = _mm_nt(h, w_qkv[:, 2 * d:].T, bm=ATTN_BLOCK)
            bias = _bias_tiles(rel_bias, ATTN_BLOCK, seq)
            lam_vecs = jnp.stack([attn_lambda_q1[j], attn_lambda_k1[j], attn_lambda_q2[j], attn_lambda_k2[j]])
            o = _attention(qk, vt, bias, lam_vecs, attn_subln_g[j], lambda_init)
            x = _mm_res(o, attn_w_o[j].astype(BF16), x)
        x = _conv_ffn(x, ffn_norm[i], ffn_w_up[i], ffn_dw_w[i], ffn_dw_b[i], ffn_w_down[i])
    return _rmsnorm(x, final_norm_g, F32).reshape(batch, seq, d)
```

```python
import functools
import math

import numpy as np
import jax
import jax.numpy as jnp
from jax import lax
from jax.experimental import pallas as pl
from jax.experimental.pallas import tpu as pltpu

F32 = jnp.float32
BF16 = jnp.bfloat16

CHUNK = 64
HEAD_DIM = 64
NUM_BUCKETS = 32
MAX_DISTANCE = 128
EPS = 1e-6
N_MIXERS = 2

V7X_VMEM_BYTES = 64 * 1024 * 1024
V7X_F32_SUBLANES = 8
V7X_LANES = 128
VMEM_TEMP_BYTES = 12 * 1024 * 1024

LOG2E = math.log2(math.e)

ATTN_BLOCK = 512
ATTN_QCHUNK = 256
V_PAD = 16
CONV_ROWS = 128
CONV_HALO = 32
FFN_CARRY = V7X_F32_SUBLANES


def _nbytes(shape, dtype):
    return int(np.prod(shape)) * jnp.dtype(dtype).itemsize


def _params(semantics, blocks, scratch=()):
    need = 2 * sum(_nbytes(s, d) for s, d in blocks) + sum(_nbytes(s, d) for s, d in scratch)
    limit = min(need + VMEM_TEMP_BYTES, V7X_VMEM_BYTES - 4 * 1024 * 1024)
    return pltpu.CompilerParams(dimension_semantics=semantics, vmem_limit_bytes=limit)


def _rmsnorm_kernel(x_ref, g_ref, o_ref):
    x = x_ref[...]
    ms = jnp.mean(x * x, axis=-1, keepdims=True)
    o_ref[...] = (x * lax.rsqrt(ms + EPS) * g_ref[...]).astype(o_ref.dtype)


def _rmsnorm(x, g, out_dtype, bm=256):
    s, d = x.shape
    return pl.pallas_call(
        _rmsnorm_kernel,
        grid=(s // bm,),
        in_specs=[pl.BlockSpec((bm, d), lambda i: (i, 0)), pl.BlockSpec((1, d), lambda i: (0, 0))],
        out_specs=pl.BlockSpec((bm, d), lambda i: (i, 0)),
        out_shape=jax.ShapeDtypeStruct((s, d), out_dtype),
        compiler_params=_params(("parallel",), [((bm, d), F32), ((bm, d), out_dtype)]),
        name="rmsnorm",
    )(x, g.reshape(1, d))


def _mm_glu_kernel(h_ref, wa_ref, wg_ref, ba_ref, bg_ref, o_ref):
    h = h_ref[...]
    a = jnp.dot(h, wa_ref[...], preferred_element_type=F32) + ba_ref[...]
    g = jnp.dot(h, wg_ref[...], preferred_element_type=F32) + bg_ref[...]
    o_ref[...] = a * jax.nn.sigmoid(g)


def _mm_glu(h, w, b, bm=1024, bn=512):
    s, k = h.shape
    n = w.shape[1] // 2
    nb = n // bn
    b = b.reshape(1, 2 * n)
    return pl.pallas_call(
        _mm_glu_kernel,
        grid=(s // bm, nb),
        in_specs=[
            pl.BlockSpec((bm, k), lambda i, j: (i, 0)),
            pl.BlockSpec((k, bn), lambda i, j: (0, j)),
            pl.BlockSpec((k, bn), lambda i, j: (0, j + nb)),
            pl.BlockSpec((1, bn), lambda i, j: (0, j)),
            pl.BlockSpec((1, bn), lambda i, j: (0, j + nb)),
        ],
        out_specs=pl.BlockSpec((bm, bn), lambda i, j: (i, j)),
        out_shape=jax.ShapeDtypeStruct((s, n), F32),
        compiler_params=_params(
            ("parallel", "parallel"),
            [((bm, k), BF16), ((k, bn), BF16), ((k, bn), BF16), ((bm, bn), F32)]),
        name="mm_glu",
    )(h, w, w, b, b)


def _mm_res_kernel(a_ref, w_ref, r_ref, o_ref):
    o_ref[...] = r_ref[...] + jnp.dot(a_ref[...], w_ref[...], preferred_element_type=F32)


def _mm_res_bias_kernel(a_ref, w_ref, b_ref, r_ref, o_ref):
    o_ref[...] = r_ref[...] + (jnp.dot(a_ref[...], w_ref[...], preferred_element_type=F32) + b_ref[...])


def _mm_res(a, w, res, bias=None, bm=1024, bn=512):
    s, k = a.shape
    n = w.shape[1]
    specs = [pl.BlockSpec((bm, k), lambda i, j: (i, 0)), pl.BlockSpec((k, bn), lambda i, j: (0, j))]
    args = [a, w]
    if bias is not None:
        specs.append(pl.BlockSpec((1, bn), lambda i, j: (0, j)))
        args.append(bias.reshape(1, n))
    specs.append(pl.BlockSpec((bm, bn), lambda i, j: (i, j)))
    args.append(res)
    return pl.pallas_call(
        _mm_res_kernel if bias is None else _mm_res_bias_kernel,
        grid=(s // bm, n // bn),
        in_specs=specs,
        out_specs=pl.BlockSpec((bm, bn), lambda i, j: (i, j)),
        out_shape=jax.ShapeDtypeStruct((s, n), F32),
        compiler_params=_params(
            ("parallel", "parallel"),
            [((bm, k), BF16), ((k, bn), BF16), ((bm, bn), F32), ((bm, bn), F32)]),
        name="mm_res",
    )(*args)


def _mm_qk_kernel(h_ref, w_ref, o_ref, *, n_scaled_blocks, scale):
    acc = jnp.dot(h_ref[...], w_ref[...], preferred_element_type=F32)
    mult = jnp.where(pl.program_id(1) < n_scaled_blocks, scale, 1.0).astype(F32)
    o_ref[...] = (acc * mult).astype(o_ref.dtype)


def _mm_qk(h, w, n_scaled, scale, bm=1024, bn=1024):
    s, k = h.shape
    n = w.shape[1]
    return pl.pallas_call(
        functools.partial(_mm_qk_kernel, n_scaled_blocks=n_scaled // bn, scale=scale),
        grid=(s // bm, n // bn),
        in_specs=[pl.BlockSpec((bm, k), lambda i, j: (i, 0)), pl.BlockSpec((k, bn), lambda i, j: (0, j))],
        out_specs=pl.BlockSpec((bm, bn), lambda i, j: (i, j)),
        out_shape=jax.ShapeDtypeStruct((s, n), BF16),
        compiler_params=_params(
            ("parallel", "parallel"), [((bm, k), BF16), ((k, bn), BF16), ((bm, bn), BF16)],
            scratch=[((bm, bn), F32)]),
        name="mm_qk",
    )(h, w)


def _mm_nt_kernel(wt_ref, b_ref, h_ref, o_ref):
    acc = lax.dot_general(wt_ref[...], h_ref[...], (((1,), (1,)), ((), ())), preferred_element_type=F32)
    o_ref[0] = (acc + b_ref[...]).astype(o_ref.dtype)


def _mm_nt(h, wt, col_bias, bm, bn):
    s, k = h.shape
    n = wt.shape[0]
    return pl.pallas_call(
        _mm_nt_kernel,
        grid=(s // bm, n // bn),
        in_specs=[
            pl.BlockSpec((bn, k), lambda i, j: (j, 0)),
            pl.BlockSpec((bn, 1), lambda i, j: (j, 0)),
            pl.BlockSpec((bm, k), lambda i, j: (i, 0)),
        ],
        out_specs=pl.BlockSpec((1, bn, bm), lambda i, j: (i, j, 0)),
        out_shape=jax.ShapeDtypeStruct((s // bm, n, bm), BF16),
        compiler_params=_params(
            ("parallel", "parallel"),
            [((bn, k), BF16), ((bn, V7X_LANES), F32), ((bm, k), BF16), ((bn, bm), BF16)],
            scratch=[((bn, bm), F32)]),
        name="mm_nt",
    )(wt, col_bias, h)


def _value_weights(w_v, n_heads):
    k, d = w_v.shape
    hd = d // n_heads
    wt = jnp.pad(w_v.T.reshape(n_heads, hd, k), ((0, 0), (0, V_PAD), (0, 0)))
    ones_row = jnp.zeros((n_heads, hd + V_PAD, 1), F32).at[:, hd, :].set(1.0)
    return wt.reshape(n_heads * (hd + V_PAD), k), ones_row.reshape(n_heads * (hd + V_PAD), 1)


def _conv_ln_kernel(cur_ref, halo_ref, w_ref, b_ref, g_ref, beta_ref, o_ref, buf_ref, y_ref, *, taps):
    rows, d = cur_ref.shape
    halo = halo_ref[...]
    buf_ref[0:CONV_HALO, :] = jnp.where(pl.program_id(0) == 0, jnp.zeros_like(halo), halo)
    buf_ref[CONV_HALO:CONV_HALO + rows, :] = cur_ref[...]
    first = CONV_HALO - (taps - 1)
    for c in range(d // V7X_LANES):
        cs = slice(c * V7X_LANES, (c + 1) * V7X_LANES)
        acc = jnp.broadcast_to(b_ref[:, cs], (rows, V7X_LANES))
        for t in range(taps):
            acc = acc + w_ref[t:t + 1, cs] * buf_ref[first + t:first + t + rows, cs]
        y_ref[:, cs] = acc
    y = y_ref[...]
    mu = jnp.mean(y, axis=-1, keepdims=True)
    yc = y - mu
    var = jnp.mean(yc * yc, axis=-1, keepdims=True)
    z = yc * lax.rsqrt(var + EPS) * g_ref[...] + beta_ref[...]
    o_ref[...] = (z * jax.nn.sigmoid(z)).astype(o_ref.dtype)


def _conv_ln(u, dw_w, dw_b, ln_g, ln_b):
    s, d = u.shape
    taps = dw_w.shape[0]
    rows = CONV_ROWS
    per = rows // CONV_HALO
    vec = lambda v: v.reshape(1, d)
    return pl.pallas_call(
        functools.partial(_conv_ln_kernel, taps=taps),
        grid=(s // rows,),
        in_specs=[
            pl.BlockSpec((rows, d), lambda i: (i, 0)),
            pl.BlockSpec((CONV_HALO, d), lambda i: (jnp.maximum(i * per - 1, 0), 0)),
            pl.BlockSpec((taps, d), lambda i: (0, 0)),
            pl.BlockSpec((1, d), lambda i: (0, 0)),
            pl.BlockSpec((1, d), lambda i: (0, 0)),
            pl.BlockSpec((1, d), lambda i: (0, 0)),
        ],
        out_specs=pl.BlockSpec((rows, d), lambda i: (i, 0)),
        out_shape=jax.ShapeDtypeStruct((s, d), BF16),
        scratch_shapes=[pltpu.VMEM((rows + CONV_HALO, d), F32), pltpu.VMEM((rows, d), F32)],
        compiler_params=_params(
            ("parallel",), [((rows, d), F32), ((CONV_HALO, d), F32), ((taps, d), F32), ((rows, d), BF16)],
            scratch=[((rows + CONV_HALO, d), F32), ((rows, d), F32)]),
        name="conv_ln",
    )(u, u, dw_w, vec(dw_b), vec(ln_g), vec(ln_b))


def _ffn_up_kernel(h_ref, wg_ref, wv_ref, cwg_ref, cwv_ref, cbg_ref, cbv_ref, o_ref, bufg_ref, bufv_ref):
    rows = h_ref.shape[0]
    c = FFN_CARRY

    @pl.when(pl.program_id(1) == 0)
    def _():
        bufg_ref[0:c, :] = jnp.zeros((c, bufg_ref.shape[1]), F32)
        bufv_ref[0:c, :] = jnp.zeros((c, bufv_ref.shape[1]), F32)

    h = h_ref[...]
    bufg_ref[c:c + rows, :] = jnp.dot(h, wg_ref[...], preferred_element_type=F32)
    bufv_ref[c:c + rows, :] = jnp.dot(h, wv_ref[...], preferred_element_type=F32)

    def conv(buf_ref, cw_ref, cb_ref):
        return (cw_ref[0:1, :] * buf_ref[c - 2:c - 2 + rows, :]
                + cw_ref[1:2, :] * buf_ref[c - 1:c - 1 + rows, :]
                + cw_ref[2:3, :] * buf_ref[c:c + rows, :]
                + cb_ref[...])

    gate = conv(bufg_ref, cwg_ref, cbg_ref)
    val = conv(bufv_ref, cwv_ref, cbv_ref)
    o_ref[...] = (gate * jax.nn.sigmoid(gate) * val).astype(o_ref.dtype)
    bufg_ref[0:c, :] = bufg_ref[rows:rows + c, :]
    bufv_ref[0:c, :] = bufv_ref[rows:rows + c, :]


def _ffn_up(h, w_up, dw_w, dw_b, bm=1024, bn=512):
    s, k = h.shape
    f = w_up.shape[1] // 2
    nb = f // bn
    taps = dw_w.shape[0]
    assert taps - 1 <= FFN_CARRY
    dw_b = dw_b.reshape(1, 2 * f)
    return pl.pallas_call(
        _ffn_up_kernel,
        grid=(nb, s // bm),
        in_specs=[
            pl.BlockSpec((bm, k), lambda j, i: (i, 0)),
            pl.BlockSpec((k, bn), lambda j, i: (0, j)),
            pl.BlockSpec((k, bn), lambda j, i: (0, j + nb)),
            pl.BlockSpec((taps, bn), lambda j, i: (0, j)),
            pl.BlockSpec((taps, bn), lambda j, i: (0, j + nb)),
            pl.BlockSpec((1, bn), lambda j, i: (0, j)),
            pl.BlockSpec((1, bn), lambda j, i: (0, j + nb)),
        ],
        out_specs=pl.BlockSpec((bm, bn), lambda j, i: (i, j)),
        out_shape=jax.ShapeDtypeStruct((s, f), BF16),
        scratch_shapes=[pltpu.VMEM((bm + FFN_CARRY, bn), F32), pltpu.VMEM((bm + FFN_CARRY, bn), F32)],
        compiler_params=_params(
            ("arbitrary", "arbitrary"),
            [((bm, k), BF16), ((k, bn), BF16), ((k, bn), BF16), ((bm, bn), BF16)],
            scratch=[((bm + FFN_CARRY, bn), F32), ((bm + FFN_CARRY, bn), F32)]),
        name="ffn_up",
    )(h, w_up, w_up, dw_w, dw_w, dw_b, dw_b)


def _t5_bucket(rel):
    half = NUM_BUCKETS // 2
    max_exact = half // 2
    ret = jnp.where(rel > 0, half, 0)
    n = jnp.abs(rel)
    nf = jnp.maximum(n, 1).astype(jnp.float32)
    large = max_exact + (jnp.log(nf / max_exact) / math.log(MAX_DISTANCE / max_exact)
                         * (half - max_exact)).astype(jnp.int32)
    large = jnp.minimum(large, half - 1)
    return ret + jnp.where(n < max_exact, n, large)


def _far_bucket(block, seq):
    n = np.arange(block + 1, seq + 1, dtype=np.float64)
    half = NUM_BUCKETS // 2
    max_exact = half // 2
    large = max_exact + (np.log(n / max_exact) / math.log(MAX_DISTANCE / max_exact) * (half - max_exact)).astype(np.int64)
    assert large.min() > half - 1, "attention block too small for a constant far-key bias"
    return half - 1


def _bias_tiles_kernel(tbl_ref, bucket_ref, allowed_ref, o_ref, *, far_bucket):
    h = pl.program_id(0)
    far = tbl_ref[h, far_bucket]
    for t in range(2):
        bucket = bucket_ref[t]
        acc = jnp.zeros(bucket.shape, F32)
        for b in range(NUM_BUCKETS):
            acc = jnp.where(bucket == b, (tbl_ref[h, b] - far) * LOG2E, acc)
        o_ref[0, t] = jnp.where(allowed_ref[t] != 0, acc, -jnp.inf)


def _bias_tiles(rel_bias, block, seq):
    n_heads = rel_bias.shape[1]
    key = jnp.arange(block, dtype=jnp.int32)[:, None]
    qry = jnp.arange(block, dtype=jnp.int32)[None, :]
    rel = jnp.stack([key - qry, key - qry - block])
    bucket = _t5_bucket(rel)
    allowed = jnp.stack([(key // CHUNK) <= (qry // CHUNK), jnp.ones((block, block), bool)]).astype(jnp.int32)
    return pl.pallas_call(
        functools.partial(_bias_tiles_kernel, far_bucket=_far_bucket(block, seq)),
        grid=(n_heads,),
        in_specs=[
            pl.BlockSpec(memory_space=pltpu.SMEM),
            pl.BlockSpec((2, block, block), lambda h: (0, 0, 0)),
            pl.BlockSpec((2, block, block), lambda h: (0, 0, 0)),
        ],
        out_specs=pl.BlockSpec((1, 2, block, block), lambda h: (h, 0, 0, 0)),
        out_shape=jax.ShapeDtypeStruct((n_heads, 2, block, block), F32),
        compiler_params=_params(
            ("parallel",),
            [((2, block, block), jnp.int32), ((2, block, block), jnp.int32), ((2, block, block), F32)]),
        name="bias_tiles",
    )(rel_bias.T, bucket, allowed)


def _attn_kernel(q_ref, k_ref, vt_ref, bias_ref, lam_ref, g_ref, o_ref, qp_ref, s_ref, m_ref, acc_ref, *,
                 lambda_init):
    blk, hd = q_ref.shape
    qi = pl.program_id(1)
    q = q_ref[...]
    lane = lax.broadcasted_iota(jnp.int32, q.shape, 1)
    zero = jnp.zeros_like(q)
    qp_ref[0] = jnp.where(lane < HEAD_DIM, q, zero)
    qp_ref[1] = jnp.where(lane >= HEAD_DIM, q, zero)
    m_ref[...] = jnp.full(m_ref.shape, -jnp.inf, F32)
    acc_ref[...] = jnp.zeros(acc_ref.shape, F32)

    def scores(c, kb):
        k = k_ref[pl.ds(pl.multiple_of(kb * blk, blk), blk), :]
        s_ref[c] = lax.dot_general(k, qp_ref[c], (((1,), (1,)), ((), ())), preferred_element_type=F32)

    def accumulate(c, kb, bias_tile):
        vt = vt_ref[kb]
        for j in range(blk // ATTN_QCHUNK):
            cs = pl.ds(j * ATTN_QCHUNK, ATTN_QCHUNK)
            s = s_ref[c, :, cs]
            if bias_tile is not None:
                s = s + bias_ref[0, bias_tile, :, cs]
            m_old = m_ref[c, :, cs]
            m_new = jnp.maximum(m_old, jnp.max(s, axis=0, keepdims=True))
            alpha = jnp.exp2(m_old - m_new)
            p = jnp.exp2(s - m_new).astype(vt.dtype)
            acc_ref[c, :, cs] = alpha * acc_ref[c, :, cs] + jnp.dot(vt, p, preferred_element_type=F32)
            m_ref[c, :, cs] = m_new

    def block_step(kb, bias_tile, has_next):
        scores(1, kb)
        accumulate(0, kb, bias_tile)
        if has_next:
            scores(0, kb + 1)
        accumulate(1, kb, bias_tile)

    scores(0, 0)

    def far_step(kb, carry):
        block_step(kb, None, True)
        return carry

    lax.fori_loop(0, qi - 1, far_step, 0)

    @pl.when(qi >= 1)
    def _():
        block_step(qi - 1, 1, True)

    block_step(qi, 0, False)

    lam_v = lam_ref[...]
    lam = (jnp.exp(jnp.sum(lam_v[0:1] * lam_v[1:2], axis=-1, keepdims=True))
           - jnp.exp(jnp.sum(lam_v[2:3] * lam_v[3:4], axis=-1, keepdims=True)) + lambda_init)
    o = (acc_ref[0, 0:hd, :] / acc_ref[0, hd:hd + 1, :]
         - lam * (acc_ref[1, 0:hd, :] / acc_ref[1, hd:hd + 1, :]))
    ms = jnp.mean(o * o, axis=0, keepdims=True)
    y = o * lax.rsqrt(ms + EPS) * g_ref[...] * (1.0 - lambda_init)
    o_ref[...] = y.T.astype(o_ref.dtype)


def _attention(qk, vt, bias, lam_vecs, subln_g, lambda_init):
    s, d2 = qk.shape
    d = d2 // 2
    nb, _, blk = vt.shape
    hd = 2 * HEAD_DIM
    hv = hd + V_PAD
    n_heads = d // hd
    return pl.pallas_call(
        functools.partial(_attn_kernel, lambda_init=lambda_init),
        grid=(n_heads, nb),
        in_specs=[
            pl.BlockSpec((blk, hd), lambda h, i: (i, h)),
            pl.BlockSpec((s, hd), lambda h, i: (0, n_heads + h)),
            pl.BlockSpec((nb, hv, blk), lambda h, i: (0, h, 0)),
            pl.BlockSpec((1, 2, blk, blk), lambda h, i: (h, 0, 0, 0)),
            pl.BlockSpec((4, HEAD_DIM), lambda h, i: (0, 0)),
            pl.BlockSpec((hd, 1), lambda h, i: (0, 0)),
        ],
        out_specs=pl.BlockSpec((blk, hd), lambda h, i: (i, h)),
        out_shape=jax.ShapeDtypeStruct((s, d), BF16),
        scratch_shapes=[
            pltpu.VMEM((2, blk, hd), BF16), pltpu.VMEM((2, blk, blk), F32),
            pltpu.VMEM((2, 1, blk), F32), pltpu.VMEM((2, hv, blk), F32)],
        compiler_params=_params(
            ("parallel", "parallel"),
            [((blk, hd), BF16), ((s, hd), BF16), ((nb, hv, blk), BF16), ((2, blk, blk), F32), ((blk, hd), BF16)],
            scratch=[((2, blk, hd), BF16), ((2, blk, blk), F32), ((2, hv, blk), F32)]),
        name="diff_attention",
    )(qk, qk, vt, bias, lam_vecs, subln_g.reshape(hd, 1))


def _conv_ffn(x, norm_g, w_up, dw_w, dw_b, w_down):
    h = _rmsnorm(x, norm_g, BF16)
    act = _ffn_up(h, w_up.astype(BF16), dw_w, dw_b)
    return _mm_res(act, w_down.astype(BF16), x)


def kernel(x, mix_norm, ffn_norm, conv_w_in, conv_b_in, conv_dw_w, conv_dw_b, conv_ln_g, conv_ln_b, conv_w_out, conv_b_out, attn_w_qkv, attn_lambda_q1, attn_lambda_k1, attn_lambda_q2, attn_lambda_k2, attn_subln_g, attn_w_o, rel_bias, ffn_w_up, ffn_dw_w, ffn_dw_b, ffn_w_down, final_norm_g):
    batch, seq, d = x.shape
    assert batch == 1
    depth = mix_norm.shape[0]
    x = x.reshape(seq, d)
    for i in range(depth):
        j = i // N_MIXERS
        h = _rmsnorm(x, mix_norm[i], BF16)
        if i % N_MIXERS == 0:
            u = _mm_glu(h, conv_w_in[j].astype(BF16), conv_b_in[j])
            c = _conv_ln(u, conv_dw_w[j], conv_dw_b[j], conv_ln_g[j], conv_ln_b[j])
            x = _mm_res(c, conv_w_out[j].astype(BF16), x, bias=conv_b_out[j])
        else:
            lambda_init = 0.8 - 0.6 * math.exp(-0.3 * i)
            w_qkv = attn_w_qkv[j].astype(BF16)
            n_heads = d // (2 * HEAD_DIM)
            qk = _mm_qk(h, w_qkv[:, :2 * d], n_scaled=d, scale=HEAD_DIM ** -0.5 * LOG2E)
            wt_v, ones_col = _value_weights(w_qkv[:, 2 * d:], n_heads)
            vt = _mm_nt(h, wt_v, ones_col, bm=ATTN_BLOCK, bn=wt_v.shape[0] // 2)
            bias = _bias_tiles(rel_bias, ATTN_BLOCK, seq)
            lam_vecs = jnp.stack([attn_lambda_q1[j], attn_lambda_k1[j], attn_lambda_q2[j], attn_lambda_k2[j]])
            o = _attention(qk, vt, bias, lam_vecs, attn_subln_g[j], lambda_init)
            x = _mm_res(o, attn_w_o[j].astype(BF16), x)
        x = _conv_ffn(x, ffn_norm[i], ffn_w_up[i], ffn_dw_w[i], ffn_dw_b[i], ffn_w_down[i])
    return _rmsnorm(x, final_norm_g, F32).reshape(batch, seq, d)
```

```python
import functools
import math

import numpy as np
import jax
import jax.numpy as jnp
from jax import lax
from jax.experimental import pallas as pl
from jax.experimental.pallas import tpu as pltpu

F32 = jnp.float32
BF16 = jnp.bfloat16

CHUNK = 64
HEAD_DIM = 64
NUM_BUCKETS = 32
MAX_DISTANCE = 128
EPS = 1e-6
N_MIXERS = 2

V7X_VMEM_BYTES = 64 * 1024 * 1024
V7X_F32_SUBLANES = 8
V7X_LANES = 128
V7X_MXU_WIDTH = 256
VMEM_TEMP_BYTES = 12 * 1024 * 1024

LOG2E = math.log2(math.e)

ATTN_BLOCK = 512
ATTN_UNROLL = 4
ATTN_QCHUNK = 256
V_PAD = 16
CONV_ROWS = 128
CONV_HALO = 32
FFN_CARRY = V7X_F32_SUBLANES


def _nbytes(shape, dtype):
    return int(np.prod(shape)) * jnp.dtype(dtype).itemsize


def _params(semantics, blocks, scratch=()):
    need = 2 * sum(_nbytes(s, d) for s, d in blocks) + sum(_nbytes(s, d) for s, d in scratch)
    limit = min(need + VMEM_TEMP_BYTES, V7X_VMEM_BYTES - 4 * 1024 * 1024)
    return pltpu.CompilerParams(dimension_semantics=semantics, vmem_limit_bytes=limit)


def _rmsnorm_kernel(x_ref, g_ref, o_ref):
    x = x_ref[...]
    ms = jnp.mean(x * x, axis=-1, keepdims=True)
    o_ref[...] = (x * lax.rsqrt(ms + EPS) * g_ref[...]).astype(o_ref.dtype)


def _rmsnorm(x, g, out_dtype, bm=256):
    s, d = x.shape
    return pl.pallas_call(
        _rmsnorm_kernel,
        grid=(s // bm,),
        in_specs=[pl.BlockSpec((bm, d), lambda i: (i, 0)), pl.BlockSpec((1, d), lambda i: (0, 0))],
        out_specs=pl.BlockSpec((bm, d), lambda i: (i, 0)),
        out_shape=jax.ShapeDtypeStruct((s, d), out_dtype),
        compiler_params=_params(("parallel",), [((bm, d), F32), ((bm, d), out_dtype)]),
        name="rmsnorm",
    )(x, g.reshape(1, d))


def _mm_glu_kernel(h_ref, wa_ref, wg_ref, ba_ref, bg_ref, o_ref):
    h = h_ref[...]
    a = jnp.dot(h, wa_ref[...], preferred_element_type=F32) + ba_ref[...]
    g = jnp.dot(h, wg_ref[...], preferred_element_type=F32) + bg_ref[...]
    o_ref[...] = a * jax.nn.sigmoid(g)


def _mm_glu(h, w, b, bm=1024, bn=512):
    s, k = h.shape
    n = w.shape[1] // 2
    nb = n // bn
    b = b.reshape(1, 2 * n)
    return pl.pallas_call(
        _mm_glu_kernel,
        grid=(s // bm, nb),
        in_specs=[
            pl.BlockSpec((bm, k), lambda i, j: (i, 0)),
            pl.BlockSpec((k, bn), lambda i, j: (0, j)),
            pl.BlockSpec((k, bn), lambda i, j: (0, j + nb)),
            pl.BlockSpec((1, bn), lambda i, j: (0, j)),
            pl.BlockSpec((1, bn), lambda i, j: (0, j + nb)),
        ],
        out_specs=pl.BlockSpec((bm, bn), lambda i, j: (i, j)),
        out_shape=jax.ShapeDtypeStruct((s, n), F32),
        compiler_params=_params(
            ("parallel", "parallel"),
            [((bm, k), BF16), ((k, bn), BF16), ((k, bn), BF16), ((bm, bn), F32)]),
        name="mm_glu",
    )(h, w, w, b, b)


def _mm_res_kernel(a_ref, w_ref, r_ref, o_ref):
    o_ref[...] = r_ref[...] + jnp.dot(a_ref[...], w_ref[...], preferred_element_type=F32)


def _mm_res_bias_kernel(a_ref, w_ref, b_ref, r_ref, o_ref):
    o_ref[...] = r_ref[...] + (jnp.dot(a_ref[...], w_ref[...], preferred_element_type=F32) + b_ref[...])


def _mm_res(a, w, res, bias=None, bm=1024, bn=512):
    s, k = a.shape
    n = w.shape[1]
    specs = [pl.BlockSpec((bm, k), lambda i, j: (i, 0)), pl.BlockSpec((k, bn), lambda i, j: (0, j))]
    args = [a, w]
    if bias is not None:
        specs.append(pl.BlockSpec((1, bn), lambda i, j: (0, j)))
        args.append(bias.reshape(1, n))
    specs.append(pl.BlockSpec((bm, bn), lambda i, j: (i, j)))
    args.append(res)
    return pl.pallas_call(
        _mm_res_kernel if bias is None else _mm_res_bias_kernel,
        grid=(s // bm, n // bn),
        in_specs=specs,
        out_specs=pl.BlockSpec((bm, bn), lambda i, j: (i, j)),
        out_shape=jax.ShapeDtypeStruct((s, n), F32),
        compiler_params=_params(
            ("parallel", "parallel"),
            [((bm, k), BF16), ((k, bn), BF16), ((bm, bn), F32), ((bm, bn), F32)]),
        name="mm_res",
    )(*args)


def _mm_qk_kernel(h_ref, w_ref, o_ref, *, n_scaled_blocks, scale):
    acc = jnp.dot(h_ref[...], w_ref[...], preferred_element_type=F32)
    mult = jnp.where(pl.program_id(1) < n_scaled_blocks, scale, 1.0).astype(F32)
    o_ref[...] = (acc * mult).astype(o_ref.dtype)


def _mm_qk(h, w, n_scaled, scale, bm=1024, bn=1024):
    s, k = h.shape
    n = w.shape[1]
    return pl.pallas_call(
        functools.partial(_mm_qk_kernel, n_scaled_blocks=n_scaled // bn, scale=scale),
        grid=(s // bm, n // bn),
        in_specs=[pl.BlockSpec((bm, k), lambda i, j: (i, 0)), pl.BlockSpec((k, bn), lambda i, j: (0, j))],
        out_specs=pl.BlockSpec((bm, bn), lambda i, j: (i, j)),
        out_shape=jax.ShapeDtypeStruct((s, n), BF16),
        compiler_params=_params(
            ("parallel", "parallel"), [((bm, k), BF16), ((k, bn), BF16), ((bm, bn), BF16)],
            scratch=[((bm, bn), F32)]),
        name="mm_qk",
    )(h, w)


def _mm_nt_kernel(wt_ref, b_ref, h_ref, o_ref):
    acc = lax.dot_general(wt_ref[...], h_ref[...], (((1,), (1,)), ((), ())), preferred_element_type=F32)
    o_ref[0] = (acc + b_ref[...]).astype(o_ref.dtype)


def _mm_nt(h, wt, col_bias, bm, bn):
    s, k = h.shape
    n = wt.shape[0]
    return pl.pallas_call(
        _mm_nt_kernel,
        grid=(s // bm, n // bn),
        in_specs=[
            pl.BlockSpec((bn, k), lambda i, j: (j, 0)),
            pl.BlockSpec((bn, 1), lambda i, j: (j, 0)),
            pl.BlockSpec((bm, k), lambda i, j: (i, 0)),
        ],
        out_specs=pl.BlockSpec((1, bn, bm), lambda i, j: (i, j, 0)),
        out_shape=jax.ShapeDtypeStruct((s // bm, n, bm), BF16),
        compiler_params=_params(
            ("parallel", "parallel"),
            [((bn, k), BF16), ((bn, V7X_LANES), F32), ((bm, k), BF16), ((bn, bm), BF16)],
            scratch=[((bn, bm), F32)]),
        name="mm_nt",
    )(wt, col_bias, h)


def _value_weights(w_v, n_heads):
    k, d = w_v.shape
    hd = d // n_heads
    wt = jnp.pad(w_v.T.reshape(n_heads, hd, k), ((0, 0), (0, V_PAD), (0, 0)))
    ones_row = jnp.zeros((n_heads, hd + V_PAD, 1), F32).at[:, hd, :].set(1.0)
    return wt.reshape(n_heads * (hd + V_PAD), k), ones_row.reshape(n_heads * (hd + V_PAD), 1)


def _conv_ln_kernel(cur_ref, halo_ref, w_ref, b_ref, g_ref, beta_ref, o_ref, buf_ref, sh_ref, y_ref, *, taps):
    rows, d = cur_ref.shape
    n_tiles = d // V7X_LANES
    sub = V7X_F32_SUBLANES
    span = sh_ref.shape[2]
    first_block = pl.program_id(0) == 0
    for c in range(n_tiles):
        cs = slice(c * V7X_LANES, (c + 1) * V7X_LANES)
        halo = halo_ref[:, cs]
        buf_ref[c, 0:CONV_HALO, :] = jnp.where(first_block, jnp.zeros_like(halo), halo)
        buf_ref[c, CONV_HALO:CONV_HALO + rows, :] = cur_ref[:, cs]
        for s in range(1, sub):
            sh_ref[s - 1, c] = buf_ref[c, s:s + span, :]

    first = CONV_HALO - (taps - 1)

    def tile_body(c, carry):
        acc = jnp.broadcast_to(b_ref[c], (rows, V7X_LANES))
        for t in range(taps):
            base, s = divmod(first + t, sub)
            base *= sub
            src = buf_ref[c, base:base + rows, :] if s == 0 else sh_ref[s - 1, c, base:base + rows, :]
            acc = acc + w_ref[c, t:t + 1, :] * src
        y_ref[c] = acc
        return carry

    lax.fori_loop(0, n_tiles, tile_body, 0)

    total = y_ref[0]
    for c in range(1, n_tiles):
        total = total + y_ref[c]
    mu = jnp.sum(total, axis=-1, keepdims=True) * (1.0 / d)
    sq = jnp.zeros((rows, V7X_LANES), F32)
    for c in range(n_tiles):
        yc = y_ref[c] - mu
        sq = sq + yc * yc
    inv = lax.rsqrt(jnp.sum(sq, axis=-1, keepdims=True) * (1.0 / d) + EPS)
    for c in range(n_tiles):
        z = (y_ref[c] - mu) * inv * g_ref[c] + beta_ref[c]
        o_ref[:, c * V7X_LANES:(c + 1) * V7X_LANES] = (z * jax.nn.sigmoid(z)).astype(o_ref.dtype)


def _conv_ln(u, dw_w, dw_b, ln_g, ln_b):
    s, d = u.shape
    taps = dw_w.shape[0]
    rows = CONV_ROWS
    per = rows // CONV_HALO
    n_tiles = d // V7X_LANES
    vec = lambda v: v.reshape(n_tiles, 1, V7X_LANES)
    w_tiles = dw_w.reshape(taps, n_tiles, V7X_LANES).transpose(1, 0, 2)
    span = rows + ((CONV_HALO - 1) // V7X_F32_SUBLANES) * V7X_F32_SUBLANES
    buf = (n_tiles, rows + CONV_HALO, V7X_LANES)
    shifted = (V7X_F32_SUBLANES - 1, n_tiles, span, V7X_LANES)
    conv_out = (n_tiles, rows, V7X_LANES)
    vec_spec = pl.BlockSpec((n_tiles, 1, V7X_LANES), lambda i: (0, 0, 0))
    return pl.pallas_call(
        functools.partial(_conv_ln_kernel, taps=taps),
        grid=(s // rows,),
        in_specs=[
            pl.BlockSpec((rows, d), lambda i: (i, 0)),
            pl.BlockSpec((CONV_HALO, d), lambda i: (jnp.maximum(i * per - 1, 0), 0)),
            pl.BlockSpec((n_tiles, taps, V7X_LANES), lambda i: (0, 0, 0)),
            vec_spec, vec_spec, vec_spec,
        ],
        out_specs=pl.BlockSpec((rows, d), lambda i: (i, 0)),
        out_shape=jax.ShapeDtypeStruct((s, d), BF16),
        scratch_shapes=[pltpu.VMEM(buf, F32), pltpu.VMEM(shifted, F32), pltpu.VMEM(conv_out, F32)],
        compiler_params=_params(
            ("parallel",), [((rows, d), F32), ((CONV_HALO, d), F32), ((taps, d), F32), ((rows, d), BF16)],
            scratch=[(buf, F32), (shifted, F32), (conv_out, F32)]),
        name="conv_ln",
    )(u, u, w_tiles, vec(dw_b), vec(ln_g), vec(ln_b))


def _ffn_up_kernel(h_ref, wg_ref, wv_ref, cwg_ref, cwv_ref, cbg_ref, cbv_ref, o_ref, bufg_ref, bufv_ref):
    rows = h_ref.shape[0]
    c = FFN_CARRY

    @pl.when(pl.program_id(1) == 0)
    def _():
        bufg_ref[0:c, :] = jnp.zeros((c, bufg_ref.shape[1]), F32)
        bufv_ref[0:c, :] = jnp.zeros((c, bufv_ref.shape[1]), F32)

    h = h_ref[...]
    bufg_ref[c:c + rows, :] = jnp.dot(h, wg_ref[...], preferred_element_type=F32)
    bufv_ref[c:c + rows, :] = jnp.dot(h, wv_ref[...], preferred_element_type=F32)

    def conv(buf_ref, cw_ref, cb_ref):
        return (cw_ref[0:1, :] * buf_ref[c - 2:c - 2 + rows, :]
                + cw_ref[1:2, :] * buf_ref[c - 1:c - 1 + rows, :]
                + cw_ref[2:3, :] * buf_ref[c:c + rows, :]
                + cb_ref[...])

    gate = conv(bufg_ref, cwg_ref, cbg_ref)
    val = conv(bufv_ref, cwv_ref, cbv_ref)
    o_ref[...] = (gate * jax.nn.sigmoid(gate) * val).astype(o_ref.dtype)
    bufg_ref[0:c, :] = bufg_ref[rows:rows + c, :]
    bufv_ref[0:c, :] = bufv_ref[rows:rows + c, :]


def _ffn_up(h, w_up, dw_w, dw_b, bm=1024, bn=512):
    s, k = h.shape
    f = w_up.shape[1] // 2
    nb = f // bn
    taps = dw_w.shape[0]
    assert taps - 1 <= FFN_CARRY
    dw_b = dw_b.reshape(1, 2 * f)
    return pl.pallas_call(
        _ffn_up_kernel,
        grid=(nb, s // bm),
        in_specs=[
            pl.BlockSpec((bm, k), lambda j, i: (i, 0)),
            pl.BlockSpec((k, bn), lambda j, i: (0, j)),
            pl.BlockSpec((k, bn), lambda j, i: (0, j + nb)),
            pl.BlockSpec((taps, bn), lambda j, i: (0, j)),
            pl.BlockSpec((taps, bn), lambda j, i: (0, j + nb)),
            pl.BlockSpec((1, bn), lambda j, i: (0, j)),
            pl.BlockSpec((1, bn), lambda j, i: (0, j + nb)),
        ],
        out_specs=pl.BlockSpec((bm, bn), lambda j, i: (i, j)),
        out_shape=jax.ShapeDtypeStruct((s, f), BF16),
        scratch_shapes=[pltpu.VMEM((bm + FFN_CARRY, bn), F32), pltpu.VMEM((bm + FFN_CARRY, bn), F32)],
        compiler_params=_params(
            ("arbitrary", "arbitrary"),
            [((bm, k), BF16), ((k, bn), BF16), ((k, bn), BF16), ((bm, bn), BF16)],
            scratch=[((bm + FFN_CARRY, bn), F32), ((bm + FFN_CARRY, bn), F32)]),
        name="ffn_up",
    )(h, w_up, w_up, dw_w, dw_w, dw_b, dw_b)


def _t5_bucket(rel):
    half = NUM_BUCKETS // 2
    max_exact = half // 2
    ret = jnp.where(rel > 0, half, 0)
    n = jnp.abs(rel)
    nf = jnp.maximum(n, 1).astype(jnp.float32)
    large = max_exact + (jnp.log(nf / max_exact) / math.log(MAX_DISTANCE / max_exact)
                         * (half - max_exact)).astype(jnp.int32)
    large = jnp.minimum(large, half - 1)
    return ret + jnp.where(n < max_exact, n, large)


def _far_bucket(block, seq):
    n = np.arange(block + 1, seq + 1, dtype=np.float64)
    half = NUM_BUCKETS // 2
    max_exact = half // 2
    large = max_exact + (np.log(n / max_exact) / math.log(MAX_DISTANCE / max_exact) * (half - max_exact)).astype(np.int64)
    assert large.min() > half - 1, "attention block too small for a constant far-key bias"
    return half - 1


def _bias_tiles_kernel(tbl_ref, bucket_ref, allowed_ref, o_ref, *, far_bucket):
    h = pl.program_id(0)
    far = tbl_ref[h, far_bucket]
    for t in range(2):
        bucket = bucket_ref[t]
        acc = jnp.zeros(bucket.shape, F32)
        for b in range(NUM_BUCKETS):
            acc = jnp.where(bucket == b, (tbl_ref[h, b] - far) * LOG2E, acc)
        o_ref[0, t] = jnp.where(allowed_ref[t] != 0, acc, -jnp.inf)


def _bias_tiles(rel_bias, block, seq):
    n_heads = rel_bias.shape[1]
    key = jnp.arange(block, dtype=jnp.int32)[:, None]
    qry = jnp.arange(block, dtype=jnp.int32)[None, :]
    rel = jnp.stack([key - qry, key - qry - block])
    bucket = _t5_bucket(rel)
    allowed = jnp.stack([(key // CHUNK) <= (qry // CHUNK), jnp.ones((block, block), bool)]).astype(jnp.int32)
    return pl.pallas_call(
        functools.partial(_bias_tiles_kernel, far_bucket=_far_bucket(block, seq)),
        grid=(n_heads,),
        in_specs=[
            pl.BlockSpec(memory_space=pltpu.SMEM),
            pl.BlockSpec((2, block, block), lambda h: (0, 0, 0)),
            pl.BlockSpec((2, block, block), lambda h: (0, 0, 0)),
        ],
        out_specs=pl.BlockSpec((1, 2, block, block), lambda h: (h, 0, 0, 0)),
        out_shape=jax.ShapeDtypeStruct((n_heads, 2, block, block), F32),
        compiler_params=_params(
            ("parallel",),
            [((2, block, block), jnp.int32), ((2, block, block), jnp.int32), ((2, block, block), F32)]),
        name="bias_tiles",
    )(rel_bias.T, bucket, allowed)


def _attn_kernel(q_ref, k_ref, vt_ref, bias_ref, lam_ref, g_ref, o_ref, qp_ref, s_ref, m_ref, acc_ref, *,
                 lambda_init):
    blk, hd = q_ref.shape
    qi = pl.program_id(1)
    q = q_ref[...]
    lane = lax.broadcasted_iota(jnp.int32, q.shape, 1)
    zero = jnp.zeros_like(q)
    qp_ref[0] = jnp.where(lane < HEAD_DIM, q, zero)
    qp_ref[1] = jnp.where(lane >= HEAD_DIM, q, zero)
    m_ref[...] = jnp.full(m_ref.shape, -jnp.inf, F32)
    acc_ref[...] = jnp.zeros(acc_ref.shape, F32)

    def scores(c, kb):
        k = k_ref[pl.ds(pl.multiple_of(kb * blk, blk), blk), :]
        s_ref[c] = lax.dot_general(k, qp_ref[c], (((1,), (1,)), ((), ())), preferred_element_type=F32)

    def accumulate(c, kb, bias_tile):
        vt = vt_ref[kb]
        for j in range(blk // ATTN_QCHUNK):
            cs = pl.ds(j * ATTN_QCHUNK, ATTN_QCHUNK)
            s = s_ref[c, :, cs]
            if bias_tile is not None:
                s = s + bias_ref[0, bias_tile, :, cs]
            m_old = m_ref[c, :, cs]
            m_new = jnp.maximum(m_old, jnp.max(s, axis=0, keepdims=True))
            alpha = jnp.exp2(m_old - m_new)
            p = jnp.exp2(s - m_new).astype(vt.dtype)
            acc_ref[c, :, cs] = alpha * acc_ref[c, :, cs] + jnp.dot(vt, p, preferred_element_type=F32)
            m_ref[c, :, cs] = m_new

    def block_step(kb, bias_tile, has_next):
        scores(1, kb)
        accumulate(0, kb, bias_tile)
        if has_next:
            scores(0, kb + 1)
        accumulate(1, kb, bias_tile)

    scores(0, 0)

    def far_group(group, carry):
        for u in range(ATTN_UNROLL):
            block_step(ATTN_UNROLL * group + u, None, True)
        return carry

    def far_single(kb, carry):
        block_step(kb, None, True)
        return carry

    n_far = jnp.maximum(qi - 1, 0)
    n_grouped = (n_far // ATTN_UNROLL) * ATTN_UNROLL
    lax.fori_loop(0, n_far // ATTN_UNROLL, far_group, 0)
    lax.fori_loop(n_grouped, n_far, far_single, 0)

    @pl.when(qi >= 1)
    def _():
        block_step(qi - 1, 1, True)

    block_step(qi, 0, False)

    lam_v = lam_ref[...]
    lam = (jnp.exp(jnp.sum(lam_v[0:1] * lam_v[1:2], axis=-1, keepdims=True))
           - jnp.exp(jnp.sum(lam_v[2:3] * lam_v[3:4], axis=-1, keepdims=True)) + lambda_init)
    o = (acc_ref[0, 0:hd, :] / acc_ref[0, hd:hd + 1, :]
         - lam * (acc_ref[1, 0:hd, :] / acc_ref[1, hd:hd + 1, :]))
    ms = jnp.mean(o * o, axis=0, keepdims=True)
    y = o * lax.rsqrt(ms + EPS) * g_ref[...] * (1.0 - lambda_init)
    o_ref[...] = y.T.astype(o_ref.dtype)


def _attention(qk, vt, bias, lam_vecs, subln_g, lambda_init):
    s, d2 = qk.shape
    d = d2 // 2
    nb, _, blk = vt.shape
    hd = 2 * HEAD_DIM
    hv = hd + V_PAD
    n_heads = d // hd
    return pl.pallas_call(
        functools.partial(_attn_kernel, lambda_init=lambda_init),
        grid=(n_heads, nb),
        in_specs=[
            pl.BlockSpec((blk, hd), lambda h, i: (i, h)),
            pl.BlockSpec((s, hd), lambda h, i: (0, n_heads + h)),
            pl.BlockSpec((nb, hv, blk), lambda h, i: (0, h, 0)),
            pl.BlockSpec((1, 2, blk, blk), lambda h, i: (h, 0, 0, 0)),
            pl.BlockSpec((4, HEAD_DIM), lambda h, i: (0, 0)),
            pl.BlockSpec((hd, 1), lambda h, i: (0, 0)),
        ],
        out_specs=pl.BlockSpec((blk, hd), lambda h, i: (i, h)),
        out_shape=jax.ShapeDtypeStruct((s, d), BF16),
        scratch_shapes=[
            pltpu.VMEM((2, blk, hd), BF16), pltpu.VMEM((2, blk, blk), F32),
            pltpu.VMEM((2, 1, blk), F32), pltpu.VMEM((2, hv, blk), F32)],
        compiler_params=_params(
            ("parallel", "parallel"),
            [((blk, hd), BF16), ((s, hd), BF16), ((nb, hv, blk), BF16), ((2, blk, blk), F32), ((blk, hd), BF16)],
            scratch=[((2, blk, hd), BF16), ((2, blk, blk), F32), ((2, hv, blk), F32)]),
        name="diff_attention",
    )(qk, qk, vt, bias, lam_vecs, subln_g.reshape(hd, 1))


def _conv_ffn(x, norm_g, w_up, dw_w, dw_b, w_down):
    h = _rmsnorm(x, norm_g, BF16)
    act = _ffn_up(h, w_up.astype(BF16), dw_w, dw_b)
    return _mm_res(act, w_down.astype(BF16), x)


def kernel(x, mix_norm, ffn_norm, conv_w_in, conv_b_in, conv_dw_w, conv_dw_b, conv_ln_g, conv_ln_b, conv_w_out, conv_b_out, attn_w_qkv, attn_lambda_q1, attn_lambda_k1, attn_lambda_q2, attn_lambda_k2, attn_subln_g, attn_w_o, rel_bias, ffn_w_up, ffn_dw_w, ffn_dw_b, ffn_w_down, final_norm_g):
    batch, seq, d = x.shape
    assert batch == 1
    depth = mix_norm.shape[0]
    x = x.reshape(seq, d)
    for i in range(depth):
        j = i // N_MIXERS
        h = _rmsnorm(x, mix_norm[i], BF16)
        if i % N_MIXERS == 0:
            u = _mm_glu(h, conv_w_in[j].astype(BF16), conv_b_in[j])
            c = _conv_ln(u, conv_dw_w[j], conv_dw_b[j], conv_ln_g[j], conv_ln_b[j])
            x = _mm_res(c, conv_w_out[j].astype(BF16), x, bias=conv_b_out[j])
        else:
            lambda_init = 0.8 - 0.6 * math.exp(-0.3 * i)
            w_qkv = attn_w_qkv[j].astype(BF16)
            n_heads = d // (2 * HEAD_DIM)
            qk = _mm_qk(h, w_qkv[:, :2 * d], n_scaled=d, scale=HEAD_DIM ** -0.5 * LOG2E)
            wt_v, ones_col = _value_weights(w_qkv[:, 2 * d:], n_heads)
            vt = _mm_nt(h, wt_v, ones_col, bm=ATTN_BLOCK, bn=wt_v.shape[0] // 2)
            bias = _bias_tiles(rel_bias, ATTN_BLOCK, seq)
            lam_vecs = jnp.stack([attn_lambda_q1[j], attn_lambda_k1[j], attn_lambda_q2[j], attn_lambda_k2[j]])
            o = _attention(qk, vt, bias, lam_vecs, attn_subln_g[j], lambda_init)
            x = _mm_res(o, attn_w_o[j].astype(BF16), x)
        x = _conv_ffn(x, ffn_norm[i], ffn_w_up[i], ffn_dw_w[i], ffn_dw_b[i], ffn_w_down[i])
    return _rmsnorm(x, final_norm_g, F32).reshape(batch, seq, d)
```

```python
import functools
import math

import numpy as np
import jax
import jax.numpy as jnp
from jax import lax
from jax.experimental import pallas as pl
from jax.experimental.pallas import tpu as pltpu

F32 = jnp.float32
BF16 = jnp.bfloat16

CHUNK = 64
HEAD_DIM = 64
NUM_BUCKETS = 32
MAX_DISTANCE = 128
EPS = 1e-6
N_MIXERS = 2

V7X_VMEM_BYTES = 64 * 1024 * 1024
V7X_F32_SUBLANES = 8
V7X_LANES = 128
V7X_MXU_WIDTH = 256
VMEM_TEMP_BYTES = 12 * 1024 * 1024

LOG2E = math.log2(math.e)

ATTN_BLOCK = 512
ATTN_UNROLL = 4
ATTN_QCHUNK = 256
V_PAD = 16
CONV_ROWS = 128
CONV_HALO = 32
FFN_CARRY = V7X_F32_SUBLANES


def _nbytes(shape, dtype):
    return int(np.prod(shape)) * jnp.dtype(dtype).itemsize


def _params(semantics, blocks, scratch=()):
    need = 2 * sum(_nbytes(s, d) for s, d in blocks) + sum(_nbytes(s, d) for s, d in scratch)
    limit = min(need + VMEM_TEMP_BYTES, V7X_VMEM_BYTES - 4 * 1024 * 1024)
    return pltpu.CompilerParams(dimension_semantics=semantics, vmem_limit_bytes=limit)


def _rmsnorm_kernel(x_ref, g_ref, o_ref):
    x = x_ref[...]
    ms = jnp.mean(x * x, axis=-1, keepdims=True)
    o_ref[...] = (x * lax.rsqrt(ms + EPS) * g_ref[...]).astype(o_ref.dtype)


def _rmsnorm(x, g, out_dtype, bm=512):
    s, d = x.shape
    return pl.pallas_call(
        _rmsnorm_kernel,
        grid=(s // bm,),
        in_specs=[pl.BlockSpec((bm, d), lambda i: (i, 0)), pl.BlockSpec((1, d), lambda i: (0, 0))],
        out_specs=pl.BlockSpec((bm, d), lambda i: (i, 0)),
        out_shape=jax.ShapeDtypeStruct((s, d), out_dtype),
        compiler_params=_params(("parallel",), [((bm, d), F32), ((bm, d), out_dtype)]),
        name="rmsnorm",
    )(x, g.reshape(1, d))


def _mm_glu_kernel(h_ref, wa_ref, wg_ref, ba_ref, bg_ref, o_ref):
    h = h_ref[...]
    a = jnp.dot(h, wa_ref[...].astype(BF16), preferred_element_type=F32) + ba_ref[...]
    g = jnp.dot(h, wg_ref[...].astype(BF16), preferred_element_type=F32) + bg_ref[...]
    o_ref[...] = a * jax.nn.sigmoid(g)


def _mm_glu(h, w, layer, b, bm=1024, bn=512):
    s, k = h.shape
    n = w.shape[2] // 2
    nb = n // bn
    b = b.reshape(1, 2 * n)
    return pl.pallas_call(
        _mm_glu_kernel,
        grid=(s // bm, nb),
        in_specs=[
            pl.BlockSpec((bm, k), lambda i, j: (i, 0)),
            pl.BlockSpec((None, k, bn), lambda i, j: (layer, 0, j)),
            pl.BlockSpec((None, k, bn), lambda i, j: (layer, 0, j + nb)),
            pl.BlockSpec((1, bn), lambda i, j: (0, j)),
            pl.BlockSpec((1, bn), lambda i, j: (0, j + nb)),
        ],
        out_specs=pl.BlockSpec((bm, bn), lambda i, j: (i, j)),
        out_shape=jax.ShapeDtypeStruct((s, n), F32),
        compiler_params=_params(
            ("parallel", "parallel"),
            [((bm, k), BF16), ((k, bn), w.dtype), ((k, bn), w.dtype), ((bm, bn), F32)]),
        name="mm_glu",
    )(h, w, w, b, b)


def _mm_res_kernel(a_ref, w_ref, r_ref, o_ref):
    o_ref[...] = r_ref[...] + jnp.dot(a_ref[...], w_ref[...].astype(BF16), preferred_element_type=F32)


def _mm_res_bias_kernel(a_ref, w_ref, b_ref, r_ref, o_ref):
    o_ref[...] = r_ref[...] + (
        jnp.dot(a_ref[...], w_ref[...].astype(BF16), preferred_element_type=F32) + b_ref[...])


def _mm_res(a, w, layer, res, bias=None, bm=1024):
    s, k = a.shape
    n = w.shape[2]
    bn = next(c for c in (1024, 512, 256)
              if 2 * (_nbytes((bm, k), BF16) + _nbytes((k, c), w.dtype) + 2 * _nbytes((bm, c), F32))
              + VMEM_TEMP_BYTES <= V7X_VMEM_BYTES - 8 * 1024 * 1024)
    specs = [pl.BlockSpec((bm, k), lambda i, j: (i, 0)),
             pl.BlockSpec((None, k, bn), lambda i, j: (layer, 0, j))]
    args = [a, w]
    if bias is not None:
        specs.append(pl.BlockSpec((1, bn), lambda i, j: (0, j)))
        args.append(bias.reshape(1, n))
    specs.append(pl.BlockSpec((bm, bn), lambda i, j: (i, j)))
    args.append(res)
    return pl.pallas_call(
        _mm_res_kernel if bias is None else _mm_res_bias_kernel,
        grid=(s // bm, n // bn),
        in_specs=specs,
        out_specs=pl.BlockSpec((bm, bn), lambda i, j: (i, j)),
        out_shape=jax.ShapeDtypeStruct((s, n), F32),
        compiler_params=_params(
            ("parallel", "parallel"),
            [((bm, k), BF16), ((k, bn), w.dtype), ((bm, bn), F32), ((bm, bn), F32)]),
        name="mm_res",
    )(*args)


def _mm_qk_kernel(h_ref, w_ref, o_ref, *, n_scaled_blocks, scale):
    acc = jnp.dot(h_ref[...], w_ref[...].astype(BF16), preferred_element_type=F32)
    mult = jnp.where(pl.program_id(1) < n_scaled_blocks, scale, 1.0).astype(F32)
    o_ref[...] = (acc * mult).astype(o_ref.dtype)


def _mm_qk(h, w, layer, n, n_scaled, scale, bm=1024, bn=1024):
    s, k = h.shape
    return pl.pallas_call(
        functools.partial(_mm_qk_kernel, n_scaled_blocks=n_scaled // bn, scale=scale),
        grid=(s // bm, n // bn),
        in_specs=[pl.BlockSpec((bm, k), lambda i, j: (i, 0)),
                  pl.BlockSpec((None, k, bn), lambda i, j: (layer, 0, j))],
        out_specs=pl.BlockSpec((bm, bn), lambda i, j: (i, j)),
        out_shape=jax.ShapeDtypeStruct((s, n), BF16),
        compiler_params=_params(
            ("parallel", "parallel"), [((bm, k), BF16), ((k, bn), w.dtype), ((bm, bn), BF16)],
            scratch=[((bm, bn), F32)]),
        name="mm_qk",
    )(h, w)


def _mm_nt_kernel(wt_ref, b_ref, h_ref, o_ref):
    acc = lax.dot_general(wt_ref[...], h_ref[...], (((1,), (1,)), ((), ())), preferred_element_type=F32)
    o_ref[0] = (acc + b_ref[...]).astype(o_ref.dtype)


def _mm_nt(h, wt, col_bias, bm, bn):
    s, k = h.shape
    n = wt.shape[0]
    return pl.pallas_call(
        _mm_nt_kernel,
        grid=(s // bm, n // bn),
        in_specs=[
            pl.BlockSpec((bn, k), lambda i, j: (j, 0)),
            pl.BlockSpec((bn, 1), lambda i, j: (j, 0)),
            pl.BlockSpec((bm, k), lambda i, j: (i, 0)),
        ],
        out_specs=pl.BlockSpec((1, bn, bm), lambda i, j: (i, j, 0)),
        out_shape=jax.ShapeDtypeStruct((s // bm, n, bm), BF16),
        compiler_params=_params(
            ("parallel", "parallel"),
            [((bn, k), BF16), ((bn, V7X_LANES), F32), ((bm, k), BF16), ((bn, bm), BF16)],
            scratch=[((bn, bm), F32)]),
        name="mm_nt",
    )(wt, col_bias, h)


def _value_weights(w_v, n_heads):
    k, d = w_v.shape
    hd = d // n_heads
    wt = jnp.pad(w_v.T.reshape(n_heads, hd, k), ((0, 0), (0, V_PAD), (0, 0)))
    ones_row = jnp.zeros((n_heads, hd + V_PAD, 1), F32).at[:, hd, :].set(1.0)
    return wt.reshape(n_heads * (hd + V_PAD), k), ones_row.reshape(n_heads * (hd + V_PAD), 1)


def _conv_ln_kernel(cur_ref, halo_ref, w_ref, b_ref, g_ref, beta_ref, o_ref, buf_ref, sh_ref, y_ref, *, taps):
    rows, d = cur_ref.shape
    n_tiles = d // V7X_LANES
    sub = V7X_F32_SUBLANES
    span = sh_ref.shape[2]
    first_block = pl.program_id(0) == 0
    for c in range(n_tiles):
        cs = slice(c * V7X_LANES, (c + 1) * V7X_LANES)
        halo = halo_ref[:, cs]
        buf_ref[c, 0:CONV_HALO, :] = jnp.where(first_block, jnp.zeros_like(halo), halo)
        buf_ref[c, CONV_HALO:CONV_HALO + rows, :] = cur_ref[:, cs]
        for s in range(1, sub):
            sh_ref[s - 1, c] = buf_ref[c, s:s + span, :]

    first = CONV_HALO - (taps - 1)

    def tile_body(c, carry):
        acc = jnp.broadcast_to(b_ref[c], (rows, V7X_LANES))
        for t in range(taps):
            base, s = divmod(first + t, sub)
            base *= sub
            src = buf_ref[c, base:base + rows, :] if s == 0 else sh_ref[s - 1, c, base:base + rows, :]
            acc = acc + w_ref[c, t:t + 1, :] * src
        y_ref[c] = acc
        return carry

    lax.fori_loop(0, n_tiles, tile_body, 0)

    total = y_ref[0]
    for c in range(1, n_tiles):
        total = total + y_ref[c]
    mu = jnp.sum(total, axis=-1, keepdims=True) * (1.0 / d)
    sq = jnp.zeros((rows, V7X_LANES), F32)
    for c in range(n_tiles):
        yc = y_ref[c] - mu
        sq = sq + yc * yc
    inv = lax.rsqrt(jnp.sum(sq, axis=-1, keepdims=True) * (1.0 / d) + EPS)
    for c in range(n_tiles):
        z = (y_ref[c] - mu) * inv * g_ref[c] + beta_ref[c]
        o_ref[:, c * V7X_LANES:(c + 1) * V7X_LANES] = (z * jax.nn.sigmoid(z)).astype(o_ref.dtype)


def _conv_ln(u, dw_w, dw_b, ln_g, ln_b):
    s, d = u.shape
    taps = dw_w.shape[0]
    rows = CONV_ROWS
    per = rows // CONV_HALO
    n_tiles = d // V7X_LANES
    vec = lambda v: v.reshape(n_tiles, 1, V7X_LANES)
    w_tiles = dw_w.reshape(taps, n_tiles, V7X_LANES).transpose(1, 0, 2)
    span = rows + ((CONV_HALO - 1) // V7X_F32_SUBLANES) * V7X_F32_SUBLANES
    buf = (n_tiles, rows + CONV_HALO, V7X_LANES)
    shifted = (V7X_F32_SUBLANES - 1, n_tiles, span, V7X_LANES)
    conv_out = (n_tiles, rows, V7X_LANES)
    vec_spec = pl.BlockSpec((n_tiles, 1, V7X_LANES), lambda i: (0, 0, 0))
    return pl.pallas_call(
        functools.partial(_conv_ln_kernel, taps=taps),
        grid=(s // rows,),
        in_specs=[
            pl.BlockSpec((rows, d), lambda i: (i, 0)),
            pl.BlockSpec((CONV_HALO, d), lambda i: (jnp.maximum(i * per - 1, 0), 0)),
            pl.BlockSpec((n_tiles, taps, V7X_LANES), lambda i: (0, 0, 0)),
            vec_spec, vec_spec, vec_spec,
        ],
        out_specs=pl.BlockSpec((rows, d), lambda i: (i, 0)),
        out_shape=jax.ShapeDtypeStruct((s, d), BF16),
        scratch_shapes=[pltpu.VMEM(buf, F32), pltpu.VMEM(shifted, F32), pltpu.VMEM(conv_out, F32)],
        compiler_params=_params(
            ("parallel",), [((rows, d), F32), ((CONV_HALO, d), F32), ((taps, d), F32), ((rows, d), BF16)],
            scratch=[(buf, F32), (shifted, F32), (conv_out, F32)]),
        name="conv_ln",
    )(u, u, w_tiles, vec(dw_b), vec(ln_g), vec(ln_b))


def _ffn_up_kernel(h_ref, wg_ref, wv_ref, cwg_ref, cwv_ref, cbg_ref, cbv_ref, o_ref,
                   wgb_ref, wvb_ref, buf_ref):
    rows = h_ref.shape[0]
    c = FFN_CARRY

    @pl.when(pl.program_id(1) == 0)
    def _():
        wgb_ref[...] = wg_ref[...].astype(BF16)
        wvb_ref[...] = wv_ref[...].astype(BF16)
        for part in range(2):
            buf_ref[part, 0:c, :] = jnp.zeros((c, buf_ref.shape[2]), F32)

    h = h_ref[...]
    buf_ref[0, c:c + rows, :] = jnp.dot(h, wgb_ref[...], preferred_element_type=F32)
    buf_ref[1, c:c + rows, :] = jnp.dot(h, wvb_ref[...], preferred_element_type=F32)

    def conv(part, cw_ref, cb_ref):
        return (cw_ref[0:1, :] * buf_ref[part, c - 2:c - 2 + rows, :]
                + cw_ref[1:2, :] * buf_ref[part, c - 1:c - 1 + rows, :]
                + cw_ref[2:3, :] * buf_ref[part, c:c + rows, :]
                + cb_ref[...])

    gate = conv(0, cwg_ref, cbg_ref)
    val = conv(1, cwv_ref, cbv_ref)
    o_ref[...] = (gate * jax.nn.sigmoid(gate) * val).astype(o_ref.dtype)
    for part in range(2):
        buf_ref[part, 0:c, :] = buf_ref[part, rows:rows + c, :]


def _ffn_up(h, w_up, layer, dw_w, dw_b, bm=1024, bn=512):
    s, k = h.shape
    f = w_up.shape[2] // 2
    nb = f // bn
    taps = dw_w.shape[0]
    assert taps - 1 <= FFN_CARRY
    dw_b = dw_b.reshape(1, 2 * f)
    buf = (2, bm + FFN_CARRY, bn)
    return pl.pallas_call(
        _ffn_up_kernel,
        grid=(nb, s // bm),
        in_specs=[
            pl.BlockSpec((bm, k), lambda j, i: (i, 0)),
            pl.BlockSpec((None, k, bn), lambda j, i: (layer, 0, j)),
            pl.BlockSpec((None, k, bn), lambda j, i: (layer, 0, j + nb)),
            pl.BlockSpec((taps, bn), lambda j, i: (0, j)),
            pl.BlockSpec((taps, bn), lambda j, i: (0, j + nb)),
            pl.BlockSpec((1, bn), lambda j, i: (0, j)),
            pl.BlockSpec((1, bn), lambda j, i: (0, j + nb)),
        ],
        out_specs=pl.BlockSpec((bm, bn), lambda j, i: (i, j)),
        out_shape=jax.ShapeDtypeStruct((s, f), BF16),
        scratch_shapes=[pltpu.VMEM((k, bn), BF16), pltpu.VMEM((k, bn), BF16), pltpu.VMEM(buf, F32)],
        compiler_params=_params(
            ("arbitrary", "arbitrary"),
            [((bm, k), BF16), ((k, bn), w_up.dtype), ((k, bn), w_up.dtype), ((bm, bn), BF16)],
            scratch=[((k, bn), BF16), ((k, bn), BF16), (buf, F32)]),
        name="ffn_up",
    )(h, w_up, w_up, dw_w, dw_w, dw_b, dw_b)


def _t5_bucket(rel):
    half = NUM_BUCKETS // 2
    max_exact = half // 2
    ret = jnp.where(rel > 0, half, 0)
    n = jnp.abs(rel)
    nf = jnp.maximum(n, 1).astype(jnp.float32)
    large = max_exact + (jnp.log(nf / max_exact) / math.log(MAX_DISTANCE / max_exact)
                         * (half - max_exact)).astype(jnp.int32)
    large = jnp.minimum(large, half - 1)
    return ret + jnp.where(n < max_exact, n, large)


def _far_bucket(block, seq):
    n = np.arange(block + 1, seq + 1, dtype=np.float64)
    half = NUM_BUCKETS // 2
    max_exact = half // 2
    large = max_exact + (np.log(n / max_exact) / math.log(MAX_DISTANCE / max_exact) * (half - max_exact)).astype(np.int64)
    assert large.min() > half - 1, "attention block too small for a constant far-key bias"
    return half - 1


def _bias_tiles_kernel(tbl_ref, bucket_ref, allowed_ref, o_ref, *, far_bucket):
    h = pl.program_id(0)
    far = tbl_ref[h, far_bucket]
    for t in range(2):
        bucket = bucket_ref[t]
        acc = jnp.zeros(bucket.shape, F32)
        for b in range(NUM_BUCKETS):
            acc = jnp.where(bucket == b, (tbl_ref[h, b] - far) * LOG2E, acc)
        o_ref[0, t] = jnp.where(allowed_ref[t] != 0, acc, -jnp.inf)


def _bias_tiles(rel_bias, block, seq):
    n_heads = rel_bias.shape[1]
    key = jnp.arange(block, dtype=jnp.int32)[:, None]
    qry = jnp.arange(block, dtype=jnp.int32)[None, :]
    rel = jnp.stack([key - qry, key - qry - block])
    bucket = _t5_bucket(rel)
    allowed = jnp.stack([(key // CHUNK) <= (qry // CHUNK), jnp.ones((block, block), bool)]).astype(jnp.int32)
    return pl.pallas_call(
        functools.partial(_bias_tiles_kernel, far_bucket=_far_bucket(block, seq)),
        grid=(n_heads,),
        in_specs=[
            pl.BlockSpec(memory_space=pltpu.SMEM),
            pl.BlockSpec((2, block, block), lambda h: (0, 0, 0)),
            pl.BlockSpec((2, block, block), lambda h: (0, 0, 0)),
        ],
        out_specs=pl.BlockSpec((1, 2, block, block), lambda h: (h, 0, 0, 0)),
        out_shape=jax.ShapeDtypeStruct((n_heads, 2, block, block), F32),
        compiler_params=_params(
            ("parallel",),
            [((2, block, block), jnp.int32), ((2, block, block), jnp.int32), ((2, block, block), F32)]),
        name="bias_tiles",
    )(rel_bias.T, bucket, allowed)


def _attn_kernel(q_ref, k_ref, vt_ref, bias_ref, lam_ref, g_ref, o_ref, qp_ref, s_ref, m_ref, acc_ref, *,
                 lambda_init):
    blk, hd = q_ref.shape
    qi = pl.program_id(1)
    q = q_ref[...]
    lane = lax.broadcasted_iota(jnp.int32, q.shape, 1)
    zero = jnp.zeros_like(q)
    qp_ref[0] = jnp.where(lane < HEAD_DIM, q, zero)
    qp_ref[1] = jnp.where(lane >= HEAD_DIM, q, zero)
    m_ref[...] = jnp.full(m_ref.shape, -jnp.inf, F32)
    acc_ref[...] = jnp.zeros(acc_ref.shape, F32)

    def scores(c, kb):
        k = k_ref[pl.ds(pl.multiple_of(kb * blk, blk), blk), :]
        s_ref[c] = lax.dot_general(k, qp_ref[c], (((1,), (1,)), ((), ())), preferred_element_type=F32)

    def accumulate(c, kb, bias_tile):
        vt = vt_ref[kb]
        for j in range(blk // ATTN_QCHUNK):
            cs = pl.ds(j * ATTN_QCHUNK, ATTN_QCHUNK)
            s = s_ref[c, :, cs]
            if bias_tile is not None:
                s = s + bias_ref[0, bias_tile, :, cs]
            m_old = m_ref[c, :, cs]
            m_new = jnp.maximum(m_old, jnp.max(s, axis=0, keepdims=True))
            alpha = jnp.exp2(m_old - m_new)
            p = jnp.exp2(s - m_new).astype(vt.dtype)
            acc_ref[c, :, cs] = alpha * acc_ref[c, :, cs] + jnp.dot(vt, p, preferred_element_type=F32)
            m_ref[c, :, cs] = m_new

    def block_step(kb, bias_tile, has_next):
        scores(1, kb)
        accumulate(0, kb, bias_tile)
        if has_next:
            scores(0, kb + 1)
        accumulate(1, kb, bias_tile)

    scores(0, 0)

    def far_group(group, carry):
        for u in range(ATTN_UNROLL):
            block_step(ATTN_UNROLL * group + u, None, True)
        return carry

    def far_single(kb, carry):
        block_step(kb, None, True)
        return carry

    n_far = jnp.maximum(qi - 1, 0)
    n_grouped = (n_far // ATTN_UNROLL) * ATTN_UNROLL
    lax.fori_loop(0, n_far // ATTN_UNROLL, far_group, 0)
    lax.fori_loop(n_grouped, n_far, far_single, 0)

    @pl.when(qi >= 1)
    def _():
        block_step(qi - 1, 1, True)
        block_step(qi, 0, False)

    @pl.when(qi == 0)
    def _():
        block_step(qi, 0, False)

    lam_v = lam_ref[...]
    lam = (jnp.exp(jnp.sum(lam_v[0:1] * lam_v[1:2], axis=-1, keepdims=True))
           - jnp.exp(jnp.sum(lam_v[2:3] * lam_v[3:4], axis=-1, keepdims=True)) + lambda_init)
    o = (acc_ref[0, 0:hd, :] / acc_ref[0, hd:hd + 1, :]
         - lam * (acc_ref[1, 0:hd, :] / acc_ref[1, hd:hd + 1, :]))
    ms = jnp.mean(o * o, axis=0, keepdims=True)
    y = o * lax.rsqrt(ms + EPS) * g_ref[...] * (1.0 - lambda_init)
    o_ref[...] = y.T.astype(o_ref.dtype)


def _attention(qk, vt, bias, lam_vecs, subln_g, lambda_init):
    s, d2 = qk.shape
    d = d2 // 2
    nb, _, blk = vt.shape
    hd = 2 * HEAD_DIM
    hv = hd + V_PAD
    n_heads = d // hd
    return pl.pallas_call(
        functools.partial(_attn_kernel, lambda_init=lambda_init),
        grid=(n_heads, nb),
        in_specs=[
            pl.BlockSpec((blk, hd), lambda h, i: (i, h)),
            pl.BlockSpec((s, hd), lambda h, i: (0, n_heads + h)),
            pl.BlockSpec((nb, hv, blk), lambda h, i: (0, h, 0)),
            pl.BlockSpec((1, 2, blk, blk), lambda h, i: (h, 0, 0, 0)),
            pl.BlockSpec((4, HEAD_DIM), lambda h, i: (0, 0)),
            pl.BlockSpec((hd, 1), lambda h, i: (0, 0)),
        ],
        out_specs=pl.BlockSpec((blk, hd), lambda h, i: (i, h)),
        out_shape=jax.ShapeDtypeStruct((s, d), BF16),
        scratch_shapes=[
            pltpu.VMEM((2, blk, hd), BF16), pltpu.VMEM((2, blk, blk), F32),
            pltpu.VMEM((2, 1, blk), F32), pltpu.VMEM((2, hv, blk), F32)],
        compiler_params=_params(
            ("parallel", "parallel"),
            [((blk, hd), BF16), ((s, hd), BF16), ((nb, hv, blk), BF16), ((2, blk, blk), F32), ((blk, hd), BF16)],
            scratch=[((2, blk, hd), BF16), ((2, blk, blk), F32), ((2, hv, blk), F32)]),
        name="diff_attention",
    )(qk, qk, vt, bias, lam_vecs, subln_g.reshape(hd, 1))


def _conv_ffn(x, norm_g, w_up, w_down, layer, dw_w, dw_b):
    h = _rmsnorm(x, norm_g, BF16)
    act = _ffn_up(h, w_up, layer, dw_w, dw_b)
    return _mm_res(act, w_down, layer, x)


def kernel(x, mix_norm, ffn_norm, conv_w_in, conv_b_in, conv_dw_w, conv_dw_b, conv_ln_g, conv_ln_b, conv_w_out, conv_b_out, attn_w_qkv, attn_lambda_q1, attn_lambda_k1, attn_lambda_q2, attn_lambda_k2, attn_subln_g, attn_w_o, rel_bias, ffn_w_up, ffn_dw_w, ffn_dw_b, ffn_w_down, final_norm_g):
    batch, seq, d = x.shape
    assert batch == 1
    depth = mix_norm.shape[0]
    x = x.reshape(seq, d)
    ffn_w_down = ffn_w_down.astype(BF16)
    for i in range(depth):
        j = i // N_MIXERS
        h = _rmsnorm(x, mix_norm[i], BF16)
        if i % N_MIXERS == 0:
            u = _mm_glu(h, conv_w_in, j, conv_b_in[j])
            c = _conv_ln(u, conv_dw_w[j], conv_dw_b[j], conv_ln_g[j], conv_ln_b[j])
            x = _mm_res(c, conv_w_out, j, x, bias=conv_b_out[j])
        else:
            lambda_init = 0.8 - 0.6 * math.exp(-0.3 * i)
            n_heads = d // (2 * HEAD_DIM)
            qk = _mm_qk(h, attn_w_qkv, j, n=2 * d, n_scaled=d, scale=HEAD_DIM ** -0.5 * LOG2E)
            wt_v, ones_col = _value_weights(attn_w_qkv[j, :, 2 * d:].astype(BF16), n_heads)
            vt = _mm_nt(h, wt_v, ones_col, bm=ATTN_BLOCK, bn=wt_v.shape[0] // 2)
            bias = _bias_tiles(rel_bias, ATTN_BLOCK, seq)
            lam_vecs = jnp.stack([attn_lambda_q1[j], attn_lambda_k1[j], attn_lambda_q2[j], attn_lambda_k2[j]])
            o = _attention(qk, vt, bias, lam_vecs, attn_subln_g[j], lambda_init)
            x = _mm_res(o, attn_w_o, j, x)
        x = _conv_ffn(x, ffn_norm[i], ffn_w_up, ffn_w_down, i, ffn_dw_w[i], ffn_dw_b[i])
    return _rmsnorm(x, final_norm_g, F32).reshape(batch, seq, d)
```

```python
import functools
import math

import numpy as np
import jax
import jax.numpy as jnp
from jax import lax
from jax.experimental import pallas as pl
from jax.experimental.pallas import tpu as pltpu

F32 = jnp.float32
BF16 = jnp.bfloat16

CHUNK = 64
HEAD_DIM = 64
NUM_BUCKETS = 32
MAX_DISTANCE = 128
EPS = 1e-6
N_MIXERS = 2

V7X_VMEM_BYTES = 64 * 1024 * 1024
V7X_F32_SUBLANES = 8
V7X_LANES = 128
V7X_MXU_WIDTH = 256
VMEM_TEMP_BYTES = 12 * 1024 * 1024

LOG2E = math.log2(math.e)

ATTN_BLOCK = 512
ATTN_UNROLL = 4
ATTN_QCHUNK = 256
V_PAD = 16
CONV_ROWS = 128
CONV_HALO = 32
FFN_CARRY = V7X_F32_SUBLANES


def _nbytes(shape, dtype):
    return int(np.prod(shape)) * jnp.dtype(dtype).itemsize


def _params(semantics, blocks, scratch=()):
    need = 2 * sum(_nbytes(s, d) for s, d in blocks) + sum(_nbytes(s, d) for s, d in scratch)
    limit = min(need + VMEM_TEMP_BYTES, V7X_VMEM_BYTES - 4 * 1024 * 1024)
    return pltpu.CompilerParams(dimension_semantics=semantics, vmem_limit_bytes=limit)


def _rmsnorm_kernel(x_ref, g_ref, o_ref):
    x = x_ref[...]
    ms = jnp.mean(x * x, axis=-1, keepdims=True)
    o_ref[...] = (x * lax.rsqrt(ms + EPS) * g_ref[...]).astype(o_ref.dtype)


def _rmsnorm(x, g, out_dtype, bm=512):
    s, d = x.shape
    return pl.pallas_call(
        _rmsnorm_kernel,
        grid=(s // bm,),
        in_specs=[pl.BlockSpec((bm, d), lambda i: (i, 0)), pl.BlockSpec((1, d), lambda i: (0, 0))],
        out_specs=pl.BlockSpec((bm, d), lambda i: (i, 0)),
        out_shape=jax.ShapeDtypeStruct((s, d), out_dtype),
        compiler_params=_params(("parallel",), [((bm, d), F32), ((bm, d), out_dtype)]),
        name="rmsnorm",
    )(x, g.reshape(1, d))


def _mm_glu_kernel(h_ref, wa_ref, wg_ref, ba_ref, bg_ref, o_ref):
    h = h_ref[...]
    a = jnp.dot(h, wa_ref[...].astype(BF16), preferred_element_type=F32) + ba_ref[...]
    g = jnp.dot(h, wg_ref[...].astype(BF16), preferred_element_type=F32) + bg_ref[...]
    o_ref[...] = a * jax.nn.sigmoid(g)


def _mm_glu(h, w, layer, b, bm=1024, bn=512):
    s, k = h.shape
    n = w.shape[2] // 2
    nb = n // bn
    b = b.reshape(1, 2 * n)
    return pl.pallas_call(
        _mm_glu_kernel,
        grid=(s // bm, nb),
        in_specs=[
            pl.BlockSpec((bm, k), lambda i, j: (i, 0)),
            pl.BlockSpec((None, k, bn), lambda i, j: (layer, 0, j)),
            pl.BlockSpec((None, k, bn), lambda i, j: (layer, 0, j + nb)),
            pl.BlockSpec((1, bn), lambda i, j: (0, j)),
            pl.BlockSpec((1, bn), lambda i, j: (0, j + nb)),
        ],
        out_specs=pl.BlockSpec((bm, bn), lambda i, j: (i, j)),
        out_shape=jax.ShapeDtypeStruct((s, n), F32),
        compiler_params=_params(
            ("parallel", "parallel"),
            [((bm, k), BF16), ((k, bn), w.dtype), ((k, bn), w.dtype), ((bm, bn), F32)]),
        name="mm_glu",
    )(h, w, w, b, b)


def _mm_res_kernel(a_ref, w_ref, r_ref, o_ref):
    o_ref[...] = r_ref[...] + jnp.dot(a_ref[...], w_ref[...], preferred_element_type=F32)


def _mm_res(a, w, layer, res, bm=1024):
    s, k = a.shape
    n = w.shape[2]
    bn = next(c for c in (1024, 512, 256)
              if 2 * (_nbytes((bm, k), BF16) + _nbytes((k, c), BF16) + 2 * _nbytes((bm, c), F32))
              + VMEM_TEMP_BYTES <= V7X_VMEM_BYTES - 8 * 1024 * 1024)
    return pl.pallas_call(
        _mm_res_kernel,
        grid=(s // bm, n // bn),
        in_specs=[
            pl.BlockSpec((bm, k), lambda i, j: (i, 0)),
            pl.BlockSpec((None, k, bn), lambda i, j: (layer, 0, j)),
            pl.BlockSpec((bm, bn), lambda i, j: (i, j)),
        ],
        out_specs=pl.BlockSpec((bm, bn), lambda i, j: (i, j)),
        out_shape=jax.ShapeDtypeStruct((s, n), F32),
        compiler_params=_params(
            ("parallel", "parallel"),
            [((bm, k), BF16), ((k, bn), BF16), ((bm, bn), F32), ((bm, bn), F32)]),
        name="mm_res",
    )(a, w, res)


def _mm_res_norm_kernel(a_ref, w_ref, *refs, has_bias):
    if has_bias:
        b_ref, r_ref, g_ref, xo_ref, ho_ref = refs
    else:
        r_ref, g_ref, xo_ref, ho_ref = refs
    y = jnp.dot(a_ref[...], w_ref[...], preferred_element_type=F32)
    if has_bias:
        y = y + b_ref[...]
    y = r_ref[...] + y
    xo_ref[...] = y
    ms = jnp.mean(y * y, axis=-1, keepdims=True)
    ho_ref[...] = (y * lax.rsqrt(ms + EPS) * g_ref[...]).astype(ho_ref.dtype)


def _mm_res_norm(a, w, layer, res, norm_g, bias=None, bm=512):
    s, k = a.shape
    n = w.shape[2]
    row_spec = lambda cols: pl.BlockSpec((bm, cols), lambda i: (i, 0))
    vec_spec = pl.BlockSpec((1, n), lambda i: (0, 0))
    specs = [row_spec(k), pl.BlockSpec((None, k, n), lambda i: (layer, 0, 0))]
    args = [a, w]
    if bias is not None:
        specs.append(vec_spec)
        args.append(bias.reshape(1, n))
    specs += [row_spec(n), vec_spec]
    args += [res, norm_g.reshape(1, n)]
    return pl.pallas_call(
        functools.partial(_mm_res_norm_kernel, has_bias=bias is not None),
        grid=(s // bm,),
        in_specs=specs,
        out_specs=[row_spec(n), row_spec(n)],
        out_shape=[jax.ShapeDtypeStruct((s, n), F32), jax.ShapeDtypeStruct((s, n), BF16)],
        compiler_params=_params(
            ("parallel",),
            [((bm, k), BF16), ((k, n), BF16), ((bm, n), F32), ((bm, n), F32), ((bm, n), BF16)]),
        name="mm_res_norm",
    )(*args)


def _mm_qk_kernel(h_ref, w_ref, o_ref, *, n_scaled_blocks, scale):
    acc = jnp.dot(h_ref[...], w_ref[...].astype(BF16), preferred_element_type=F32)
    mult = jnp.where(pl.program_id(1) < n_scaled_blocks, scale, 1.0).astype(F32)
    o_ref[...] = (acc * mult).astype(o_ref.dtype)


def _mm_qk(h, w, layer, n, n_scaled, scale, bm=1024, bn=1024):
    s, k = h.shape
    return pl.pallas_call(
        functools.partial(_mm_qk_kernel, n_scaled_blocks=n_scaled // bn, scale=scale),
        grid=(s // bm, n // bn),
        in_specs=[pl.BlockSpec((bm, k), lambda i, j: (i, 0)),
                  pl.BlockSpec((None, k, bn), lambda i, j: (layer, 0, j))],
        out_specs=pl.BlockSpec((bm, bn), lambda i, j: (i, j)),
        out_shape=jax.ShapeDtypeStruct((s, n), BF16),
        compiler_params=_params(
            ("parallel", "parallel"), [((bm, k), BF16), ((k, bn), w.dtype), ((bm, bn), BF16)],
            scratch=[((bm, bn), F32)]),
        name="mm_qk",
    )(h, w)


def _mm_nt_kernel(wt_ref, b_ref, h_ref, o_ref):
    acc = lax.dot_general(wt_ref[...], h_ref[...], (((1,), (1,)), ((), ())), preferred_element_type=F32)
    o_ref[0] = (acc + b_ref[...]).astype(o_ref.dtype)


def _mm_nt(h, wt, col_bias, bm, bn):
    s, k = h.shape
    n = wt.shape[0]
    return pl.pallas_call(
        _mm_nt_kernel,
        grid=(s // bm, n // bn),
        in_specs=[
            pl.BlockSpec((bn, k), lambda i, j: (j, 0)),
            pl.BlockSpec((bn, 1), lambda i, j: (j, 0)),
            pl.BlockSpec((bm, k), lambda i, j: (i, 0)),
        ],
        out_specs=pl.BlockSpec((1, bn, bm), lambda i, j: (i, j, 0)),
        out_shape=jax.ShapeDtypeStruct((s // bm, n, bm), BF16),
        compiler_params=_params(
            ("parallel", "parallel"),
            [((bn, k), BF16), ((bn, V7X_LANES), F32), ((bm, k), BF16), ((bn, bm), BF16)],
            scratch=[((bn, bm), F32)]),
        name="mm_nt",
    )(wt, col_bias, h)


def _value_weights(w_v, n_heads):
    k, d = w_v.shape
    hd = d // n_heads
    wt = jnp.pad(w_v.T.reshape(n_heads, hd, k), ((0, 0), (0, V_PAD), (0, 0)))
    ones_row = jnp.zeros((n_heads, hd + V_PAD, 1), F32).at[:, hd, :].set(1.0)
    return wt.reshape(n_heads * (hd + V_PAD), k), ones_row.reshape(n_heads * (hd + V_PAD), 1)


def _conv_ln_kernel(cur_ref, halo_ref, w_ref, b_ref, g_ref, beta_ref, o_ref, buf_ref, sh_ref, y_ref, *, taps):
    rows, d = cur_ref.shape
    n_tiles = d // V7X_LANES
    sub = V7X_F32_SUBLANES
    span = sh_ref.shape[2]
    first_block = pl.program_id(0) == 0
    for c in range(n_tiles):
        cs = slice(c * V7X_LANES, (c + 1) * V7X_LANES)
        halo = halo_ref[:, cs]
        buf_ref[c, 0:CONV_HALO, :] = jnp.where(first_block, jnp.zeros_like(halo), halo)
        buf_ref[c, CONV_HALO:CONV_HALO + rows, :] = cur_ref[:, cs]
        for s in range(1, sub):
            sh_ref[s - 1, c] = buf_ref[c, s:s + span, :]

    first = CONV_HALO - (taps - 1)

    def tile_body(c, carry):
        acc = jnp.broadcast_to(b_ref[c], (rows, V7X_LANES))
        for t in range(taps):
            base, s = divmod(first + t, sub)
            base *= sub
            src = buf_ref[c, base:base + rows, :] if s == 0 else sh_ref[s - 1, c, base:base + rows, :]
            acc = acc + w_ref[c, t:t + 1, :] * src
        y_ref[c] = acc
        return carry

    lax.fori_loop(0, n_tiles, tile_body, 0)

    total = y_ref[0]
    for c in range(1, n_tiles):
        total = total + y_ref[c]
    mu = jnp.sum(total, axis=-1, keepdims=True) * (1.0 / d)
    sq = jnp.zeros((rows, V7X_LANES), F32)
    for c in range(n_tiles):
        yc = y_ref[c] - mu
        sq = sq + yc * yc
    inv = lax.rsqrt(jnp.sum(sq, axis=-1, keepdims=True) * (1.0 / d) + EPS)
    for c in range(n_tiles):
        z = (y_ref[c] - mu) * inv * g_ref[c] + beta_ref[c]
        o_ref[:, c * V7X_LANES:(c + 1) * V7X_LANES] = (z * jax.nn.sigmoid(z)).astype(o_ref.dtype)


def _conv_ln(u, dw_w, dw_b, ln_g, ln_b):
    s, d = u.shape
    taps = dw_w.shape[0]
    rows = CONV_ROWS
    per = rows // CONV_HALO
    n_tiles = d // V7X_LANES
    vec = lambda v: v.reshape(n_tiles, 1, V7X_LANES)
    w_tiles = dw_w.reshape(taps, n_tiles, V7X_LANES).transpose(1, 0, 2)
    span = rows + ((CONV_HALO - 1) // V7X_F32_SUBLANES) * V7X_F32_SUBLANES
    buf = (n_tiles, rows + CONV_HALO, V7X_LANES)
    shifted = (V7X_F32_SUBLANES - 1, n_tiles, span, V7X_LANES)
    conv_out = (n_tiles, rows, V7X_LANES)
    vec_spec = pl.BlockSpec((n_tiles, 1, V7X_LANES), lambda i: (0, 0, 0))
    return pl.pallas_call(
        functools.partial(_conv_ln_kernel, taps=taps),
        grid=(s // rows,),
        in_specs=[
            pl.BlockSpec((rows, d), lambda i: (i, 0)),
            pl.BlockSpec((CONV_HALO, d), lambda i: (jnp.maximum(i * per - 1, 0), 0)),
            pl.BlockSpec((n_tiles, taps, V7X_LANES), lambda i: (0, 0, 0)),
            vec_spec, vec_spec, vec_spec,
        ],
        out_specs=pl.BlockSpec((rows, d), lambda i: (i, 0)),
        out_shape=jax.ShapeDtypeStruct((s, d), BF16),
        scratch_shapes=[pltpu.VMEM(buf, F32), pltpu.VMEM(shifted, F32), pltpu.VMEM(conv_out, F32)],
        compiler_params=_params(
            ("parallel",), [((rows, d), F32), ((CONV_HALO, d), F32), ((taps, d), F32), ((rows, d), BF16)],
            scratch=[(buf, F32), (shifted, F32), (conv_out, F32)]),
        name="conv_ln",
    )(u, u, w_tiles, vec(dw_b), vec(ln_g), vec(ln_b))


def _ffn_up_kernel(h_ref, wg_ref, wv_ref, cwg_ref, cwv_ref, cbg_ref, cbv_ref, o_ref,
                   wgb_ref, wvb_ref, buf_ref):
    rows = h_ref.shape[0]
    c = FFN_CARRY

    @pl.when(pl.program_id(1) == 0)
    def _():
        wgb_ref[...] = wg_ref[...].astype(BF16)
        wvb_ref[...] = wv_ref[...].astype(BF16)
        for part in range(2):
            buf_ref[part, 0:c, :] = jnp.zeros((c, buf_ref.shape[2]), F32)

    h = h_ref[...]
    buf_ref[0, c:c + rows, :] = jnp.dot(h, wgb_ref[...], preferred_element_type=F32)
    buf_ref[1, c:c + rows, :] = jnp.dot(h, wvb_ref[...], preferred_element_type=F32)

    def conv(part, cw_ref, cb_ref):
        return (cw_ref[0:1, :] * buf_ref[part, c - 2:c - 2 + rows, :]
                + cw_ref[1:2, :] * buf_ref[part, c - 1:c - 1 + rows, :]
                + cw_ref[2:3, :] * buf_ref[part, c:c + rows, :]
                + cb_ref[...])

    gate = conv(0, cwg_ref, cbg_ref)
    val = conv(1, cwv_ref, cbv_ref)
    o_ref[...] = (gate * jax.nn.sigmoid(gate) * val).astype(o_ref.dtype)
    for part in range(2):
        buf_ref[part, 0:c, :] = buf_ref[part, rows:rows + c, :]


def _ffn_up(h, w_up, layer, dw_w, dw_b, bm=1024, bn=512):
    s, k = h.shape
    f = w_up.shape[2] // 2
    nb = f // bn
    taps = dw_w.shape[0]
    assert taps - 1 <= FFN_CARRY
    dw_b = dw_b.reshape(1, 2 * f)
    buf = (2, bm + FFN_CARRY, bn)
    return pl.pallas_call(
        _ffn_up_kernel,
        grid=(nb, s // bm),
        in_specs=[
            pl.BlockSpec((bm, k), lambda j, i: (i, 0)),
            pl.BlockSpec((None, k, bn), lambda j, i: (layer, 0, j)),
            pl.BlockSpec((None, k, bn), lambda j, i: (layer, 0, j + nb)),
            pl.BlockSpec((taps, bn), lambda j, i: (0, j)),
            pl.BlockSpec((taps, bn), lambda j, i: (0, j + nb)),
            pl.BlockSpec((1, bn), lambda j, i: (0, j)),
            pl.BlockSpec((1, bn), lambda j, i: (0, j + nb)),
        ],
        out_specs=pl.BlockSpec((bm, bn), lambda j, i: (i, j)),
        out_shape=jax.ShapeDtypeStruct((s, f), BF16),
        scratch_shapes=[pltpu.VMEM((k, bn), BF16), pltpu.VMEM((k, bn), BF16), pltpu.VMEM(buf, F32)],
        compiler_params=_params(
            ("arbitrary", "arbitrary"),
            [((bm, k), BF16), ((k, bn), w_up.dtype), ((k, bn), w_up.dtype), ((bm, bn), BF16)],
            scratch=[((k, bn), BF16), ((k, bn), BF16), (buf, F32)]),
        name="ffn_up",
    )(h, w_up, w_up, dw_w, dw_w, dw_b, dw_b)


def _t5_bucket(rel):
    half = NUM_BUCKETS // 2
    max_exact = half // 2
    ret = jnp.where(rel > 0, half, 0)
    n = jnp.abs(rel)
    nf = jnp.maximum(n, 1).astype(jnp.float32)
    large = max_exact + (jnp.log(nf / max_exact) / math.log(MAX_DISTANCE / max_exact)
                         * (half - max_exact)).astype(jnp.int32)
    large = jnp.minimum(large, half - 1)
    return ret + jnp.where(n < max_exact, n, large)


def _far_bucket(block, seq):
    n = np.arange(block + 1, seq + 1, dtype=np.float64)
    half = NUM_BUCKETS // 2
    max_exact = half // 2
    large = max_exact + (np.log(n / max_exact) / math.log(MAX_DISTANCE / max_exact) * (half - max_exact)).astype(np.int64)
    assert large.min() > half - 1, "attention block too small for a constant far-key bias"
    return half - 1


def _bias_tiles_kernel(tbl_ref, bucket_ref, allowed_ref, o_ref, *, far_bucket):
    h = pl.program_id(0)
    far = tbl_ref[h, far_bucket]
    for t in range(2):
        bucket = bucket_ref[t]
        acc = jnp.zeros(bucket.shape, F32)
        for b in range(NUM_BUCKETS):
            acc = jnp.where(bucket == b, (tbl_ref[h, b] - far) * LOG2E, acc)
        o_ref[0, t] = jnp.where(allowed_ref[t] != 0, acc, -jnp.inf)


def _bias_tiles(rel_bias, block, seq):
    n_heads = rel_bias.shape[1]
    key = jnp.arange(block, dtype=jnp.int32)[:, None]
    qry = jnp.arange(block, dtype=jnp.int32)[None, :]
    rel = jnp.stack([key - qry, key - qry - block])
    bucket = _t5_bucket(rel)
    allowed = jnp.stack([(key // CHUNK) <= (qry // CHUNK), jnp.ones((block, block), bool)]).astype(jnp.int32)
    return pl.pallas_call(
        functools.partial(_bias_tiles_kernel, far_bucket=_far_bucket(block, seq)),
        grid=(n_heads,),
        in_specs=[
            pl.BlockSpec(memory_space=pltpu.SMEM),
            pl.BlockSpec((2, block, block), lambda h: (0, 0, 0)),
            pl.BlockSpec((2, block, block), lambda h: (0, 0, 0)),
        ],
        out_specs=pl.BlockSpec((1, 2, block, block), lambda h: (h, 0, 0, 0)),
        out_shape=jax.ShapeDtypeStruct((n_heads, 2, block, block), F32),
        compiler_params=_params(
            ("parallel",),
            [((2, block, block), jnp.int32), ((2, block, block), jnp.int32), ((2, block, block), F32)]),
        name="bias_tiles",
    )(rel_bias.T, bucket, allowed)


def _attn_kernel(q_ref, k_ref, vt_ref, bias_ref, lam_ref, g_ref, o_ref, qp_ref, s_ref, m_ref, acc_ref, *,
                 lambda_init):
    blk, hd = q_ref.shape
    qi = pl.program_id(1)
    q = q_ref[...]
    lane = lax.broadcasted_iota(jnp.int32, q.shape, 1)
    zero = jnp.zeros_like(q)
    qp_ref[0] = jnp.where(lane < HEAD_DIM, q, zero)
    qp_ref[1] = jnp.where(lane >= HEAD_DIM, q, zero)
    m_ref[...] = jnp.full(m_ref.shape, -jnp.inf, F32)
    acc_ref[...] = jnp.zeros(acc_ref.shape, F32)

    def scores(c, kb):
        k = k_ref[pl.ds(pl.multiple_of(kb * blk, blk), blk), :]
        s_ref[c] = lax.dot_general(k, qp_ref[c], (((1,), (1,)), ((), ())), preferred_element_type=F32)

    def accumulate(c, kb, bias_tile):
        vt = vt_ref[kb]
        for j in range(blk // ATTN_QCHUNK):
            cs = pl.ds(j * ATTN_QCHUNK, ATTN_QCHUNK)
            s = s_ref[c, :, cs]
            if bias_tile is not None:
                s = s + bias_ref[0, bias_tile, :, cs]
            m_old = m_ref[c, :, cs]
            m_new = jnp.maximum(m_old, jnp.max(s, axis=0, keepdims=True))
            alpha = jnp.exp2(m_old - m_new)
            p = jnp.exp2(s - m_new).astype(vt.dtype)
            acc_ref[c, :, cs] = alpha * acc_ref[c, :, cs] + jnp.dot(vt, p, preferred_element_type=F32)
            m_ref[c, :, cs] = m_new

    def block_step(kb, bias_tile, has_next):
        scores(1, kb)
        accumulate(0, kb, bias_tile)
        if has_next:
            scores(0, kb + 1)
        accumulate(1, kb, bias_tile)

    scores(0, 0)

    def far_group(group, carry):
        for u in range(ATTN_UNROLL):
            block_step(ATTN_UNROLL * group + u, None, True)
        return carry

    def far_single(kb, carry):
        block_step(kb, None, True)
        return carry

    n_far = jnp.maximum(qi - 1, 0)
    n_grouped = (n_far // ATTN_UNROLL) * ATTN_UNROLL
    lax.fori_loop(0, n_far // ATTN_UNROLL, far_group, 0)
    lax.fori_loop(n_grouped, n_far, far_single, 0)

    @pl.when(qi >= 1)
    def _():
        block_step(qi - 1, 1, True)
        block_step(qi, 0, False)

    @pl.when(qi == 0)
    def _():
        block_step(qi, 0, False)

    lam_v = lam_ref[...]
    lam = (jnp.exp(jnp.sum(lam_v[0:1] * lam_v[1:2], axis=-1, keepdims=True))
           - jnp.exp(jnp.sum(lam_v[2:3] * lam_v[3:4], axis=-1, keepdims=True)) + lambda_init)
    o = (acc_ref[0, 0:hd, :] / acc_ref[0, hd:hd + 1, :]
         - lam * (acc_ref[1, 0:hd, :] / acc_ref[1, hd:hd + 1, :]))
    ms = jnp.mean(o * o, axis=0, keepdims=True)
    y = o * lax.rsqrt(ms + EPS) * g_ref[...] * (1.0 - lambda_init)
    o_ref[...] = y.T.astype(o_ref.dtype)


def _attention(qk, vt, bias, lam_vecs, subln_g, lambda_init):
    s, d2 = qk.shape
    d = d2 // 2
    nb, _, blk = vt.shape
    hd = 2 * HEAD_DIM
    hv = hd + V_PAD
    n_heads = d // hd
    return pl.pallas_call(
        functools.partial(_attn_kernel, lambda_init=lambda_init),
        grid=(n_heads, nb),
        in_specs=[
            pl.BlockSpec((blk, hd), lambda h, i: (i, h)),
            pl.BlockSpec((s, hd), lambda h, i: (0, n_heads + h)),
            pl.BlockSpec((nb, hv, blk), lambda h, i: (0, h, 0)),
            pl.BlockSpec((1, 2, blk, blk), lambda h, i: (h, 0, 0, 0)),
            pl.BlockSpec((4, HEAD_DIM), lambda h, i: (0, 0)),
            pl.BlockSpec((hd, 1), lambda h, i: (0, 0)),
        ],
        out_specs=pl.BlockSpec((blk, hd), lambda h, i: (i, h)),
        out_shape=jax.ShapeDtypeStruct((s, d), BF16),
        scratch_shapes=[
            pltpu.VMEM((2, blk, hd), BF16), pltpu.VMEM((2, blk, blk), F32),
            pltpu.VMEM((2, 1, blk), F32), pltpu.VMEM((2, hv, blk), F32)],
        compiler_params=_params(
            ("parallel", "parallel"),
            [((blk, hd), BF16), ((s, hd), BF16), ((nb, hv, blk), BF16), ((2, blk, blk), F32), ((blk, hd), BF16)],
            scratch=[((2, blk, hd), BF16), ((2, blk, blk), F32), ((2, hv, blk), F32)]),
        name="diff_attention",
    )(qk, qk, vt, bias, lam_vecs, subln_g.reshape(hd, 1))


def kernel(x, mix_norm, ffn_norm, conv_w_in, conv_b_in, conv_dw_w, conv_dw_b, conv_ln_g, conv_ln_b, conv_w_out, conv_b_out, attn_w_qkv, attn_lambda_q1, attn_lambda_k1, attn_lambda_q2, attn_lambda_k2, attn_subln_g, attn_w_o, rel_bias, ffn_w_up, ffn_dw_w, ffn_dw_b, ffn_w_down, final_norm_g):
    batch, seq, d = x.shape
    assert batch == 1
    depth = mix_norm.shape[0]
    x = x.reshape(seq, d)
    ffn_w_down = ffn_w_down.astype(BF16)
    conv_w_out = conv_w_out.astype(BF16)
    attn_w_o = attn_w_o.astype(BF16)
    for i in range(depth):
        j = i // N_MIXERS
        h = _rmsnorm(x, mix_norm[i], BF16)
        if i % N_MIXERS == 0:
            u = _mm_glu(h, conv_w_in, j, conv_b_in[j])
            c = _conv_ln(u, conv_dw_w[j], conv_dw_b[j], conv_ln_g[j], conv_ln_b[j])
            x, h = _mm_res_norm(c, conv_w_out, j, x, ffn_norm[i], bias=conv_b_out[j])
        else:
            lambda_init = 0.8 - 0.6 * math.exp(-0.3 * i)
            n_heads = d // (2 * HEAD_DIM)
            qk = _mm_qk(h, attn_w_qkv, j, n=2 * d, n_scaled=d, scale=HEAD_DIM ** -0.5 * LOG2E)
            wt_v, ones_col = _value_weights(attn_w_qkv[j, :, 2 * d:].astype(BF16), n_heads)
            vt = _mm_nt(h, wt_v, ones_col, bm=ATTN_BLOCK, bn=wt_v.shape[0] // 2)
            bias = _bias_tiles(rel_bias, ATTN_BLOCK, seq)
            lam_vecs = jnp.stack([attn_lambda_q1[j], attn_lambda_k1[j], attn_lambda_q2[j], attn_lambda_k2[j]])
            o = _attention(qk, vt, bias, lam_vecs, attn_subln_g[j], lambda_init)
            x, h = _mm_res_norm(o, attn_w_o, j, x, ffn_norm[i])
        act = _ffn_up(h, ffn_w_up, i, ffn_dw_w[i], ffn_dw_b[i])
        x = _mm_res(act, ffn_w_down, i, x)
    return _rmsnorm(x, final_norm_g, F32).reshape(batch, seq, d)
```

```python
import functools
import math

import numpy as np
import jax
import jax.numpy as jnp
from jax import lax
from jax.experimental import pallas as pl
from jax.experimental.pallas import tpu as pltpu

F32 = jnp.float32
BF16 = jnp.bfloat16

CHUNK = 64
HEAD_DIM = 64
NUM_BUCKETS = 32
MAX_DISTANCE = 128
EPS = 1e-6
N_MIXERS = 2

V7X_VMEM_BYTES = 64 * 1024 * 1024
V7X_F32_SUBLANES = 8
V7X_LANES = 128
V7X_MXU_WIDTH = 256
VMEM_TEMP_BYTES = 12 * 1024 * 1024

LOG2E = math.log2(math.e)

ATTN_BLOCK = 512
ATTN_UNROLL = 4
ATTN_QCHUNK = 256
V_PAD = 16
CONV_ROWS = 256
CONV_HALO = 32
FFN_CARRY = V7X_F32_SUBLANES


def _nbytes(shape, dtype):
    return int(np.prod(shape)) * jnp.dtype(dtype).itemsize


def _params(semantics, blocks, scratch=()):
    need = 2 * sum(_nbytes(s, d) for s, d in blocks) + sum(_nbytes(s, d) for s, d in scratch)
    limit = min(need + VMEM_TEMP_BYTES, V7X_VMEM_BYTES - 4 * 1024 * 1024)
    return pltpu.CompilerParams(dimension_semantics=semantics, vmem_limit_bytes=limit)


def _rmsnorm_kernel(x_ref, g_ref, o_ref):
    x = x_ref[...]
    ms = jnp.mean(x * x, axis=-1, keepdims=True)
    o_ref[...] = (x * lax.rsqrt(ms + EPS) * g_ref[...]).astype(o_ref.dtype)


def _rmsnorm(x, g, out_dtype, bm=512):
    s, d = x.shape
    return pl.pallas_call(
        _rmsnorm_kernel,
        grid=(s // bm,),
        in_specs=[pl.BlockSpec((bm, d), lambda i: (i, 0)), pl.BlockSpec((1, d), lambda i: (0, 0))],
        out_specs=pl.BlockSpec((bm, d), lambda i: (i, 0)),
        out_shape=jax.ShapeDtypeStruct((s, d), out_dtype),
        compiler_params=_params(("parallel",), [((bm, d), F32), ((bm, d), out_dtype)]),
        name="rmsnorm",
    )(x, g.reshape(1, d))


def _mm_glu_kernel(x_ref, ng_ref, wa_ref, wg_ref, ba_ref, bg_ref, o_ref, h_ref):
    @pl.when(pl.program_id(1) == 0)
    def _():
        x = x_ref[...]
        ms = jnp.mean(x * x, axis=-1, keepdims=True)
        h_ref[...] = (x * lax.rsqrt(ms + EPS) * ng_ref[...]).astype(h_ref.dtype)

    h = h_ref[...]
    a = jnp.dot(h, wa_ref[...].astype(BF16), preferred_element_type=F32) + ba_ref[...]
    g = jnp.dot(h, wg_ref[...].astype(BF16), preferred_element_type=F32) + bg_ref[...]
    o_ref[...] = a * jax.nn.sigmoid(g)


def _mm_glu(x, norm_g, w, layer, b, bm=1024, bn=512):
    s, k = x.shape
    n = w.shape[2] // 2
    nb = n // bn
    b = b.reshape(1, 2 * n)
    return pl.pallas_call(
        _mm_glu_kernel,
        grid=(s // bm, nb),
        in_specs=[
            pl.BlockSpec((bm, k), lambda i, j: (i, 0)),
            pl.BlockSpec((1, k), lambda i, j: (0, 0)),
            pl.BlockSpec((None, k, bn), lambda i, j: (layer, 0, j)),
            pl.BlockSpec((None, k, bn), lambda i, j: (layer, 0, j + nb)),
            pl.BlockSpec((1, bn), lambda i, j: (0, j)),
            pl.BlockSpec((1, bn), lambda i, j: (0, j + nb)),
        ],
        out_specs=pl.BlockSpec((bm, bn), lambda i, j: (i, j)),
        out_shape=jax.ShapeDtypeStruct((s, n), F32),
        scratch_shapes=[pltpu.VMEM((bm, k), BF16)],
        compiler_params=_params(
            ("parallel", "arbitrary"),
            [((bm, k), F32), ((k, bn), w.dtype), ((k, bn), w.dtype), ((bm, bn), F32)],
            scratch=[((bm, k), BF16)]),
        name="mm_glu",
    )(x, norm_g.reshape(1, k), w, w, b, b)


def _mm_res_kernel(a_ref, w_ref, r_ref, o_ref):
    o_ref[...] = r_ref[...] + jnp.dot(a_ref[...], w_ref[...], preferred_element_type=F32)


def _mm_res(a, w, layer, res, bm=1024):
    s, k = a.shape
    n = w.shape[2]
    bn = next(c for c in (1024, 512, 256)
              if 2 * (_nbytes((bm, k), BF16) + _nbytes((k, c), BF16) + 2 * _nbytes((bm, c), F32))
              + VMEM_TEMP_BYTES <= V7X_VMEM_BYTES - 8 * 1024 * 1024)
    return pl.pallas_call(
        _mm_res_kernel,
        grid=(s // bm, n // bn),
        in_specs=[
            pl.BlockSpec((bm, k), lambda i, j: (i, 0)),
            pl.BlockSpec((None, k, bn), lambda i, j: (layer, 0, j)),
            pl.BlockSpec((bm, bn), lambda i, j: (i, j)),
        ],
        out_specs=pl.BlockSpec((bm, bn), lambda i, j: (i, j)),
        out_shape=jax.ShapeDtypeStruct((s, n), F32),
        compiler_params=_params(
            ("parallel", "parallel"),
            [((bm, k), BF16), ((k, bn), BF16), ((bm, bn), F32), ((bm, bn), F32)]),
        name="mm_res",
    )(a, w, res)


def _mm_res_norm_kernel(a_ref, w_ref, *refs, has_bias):
    if has_bias:
        b_ref, r_ref, g_ref, xo_ref, ho_ref = refs
    else:
        r_ref, g_ref, xo_ref, ho_ref = refs
    y = jnp.dot(a_ref[...], w_ref[...], preferred_element_type=F32)
    if has_bias:
        y = y + b_ref[...]
    y = r_ref[...] + y
    xo_ref[...] = y
    ms = jnp.mean(y * y, axis=-1, keepdims=True)
    ho_ref[...] = (y * lax.rsqrt(ms + EPS) * g_ref[...]).astype(ho_ref.dtype)


def _mm_res_norm(a, w, layer, res, norm_g, bias=None, bm=512):
    s, k = a.shape
    n = w.shape[2]
    row_spec = lambda cols: pl.BlockSpec((bm, cols), lambda i: (i, 0))
    vec_spec = pl.BlockSpec((1, n), lambda i: (0, 0))
    specs = [row_spec(k), pl.BlockSpec((None, k, n), lambda i: (layer, 0, 0))]
    args = [a, w]
    if bias is not None:
        specs.append(vec_spec)
        args.append(bias.reshape(1, n))
    specs += [row_spec(n), vec_spec]
    args += [res, norm_g.reshape(1, n)]
    return pl.pallas_call(
        functools.partial(_mm_res_norm_kernel, has_bias=bias is not None),
        grid=(s // bm,),
        in_specs=specs,
        out_specs=[row_spec(n), row_spec(n)],
        out_shape=[jax.ShapeDtypeStruct((s, n), F32), jax.ShapeDtypeStruct((s, n), BF16)],
        compiler_params=_params(
            ("parallel",),
            [((bm, k), BF16), ((k, n), BF16), ((bm, n), F32), ((bm, n), F32), ((bm, n), BF16)]),
        name="mm_res_norm",
    )(*args)


def _mm_qk_kernel(h_ref, w_ref, o_ref, *, n_scaled_blocks, scale):
    acc = jnp.dot(h_ref[...], w_ref[...].astype(BF16), preferred_element_type=F32)
    mult = jnp.where(pl.program_id(1) < n_scaled_blocks, scale, 1.0).astype(F32)
    o_ref[...] = (acc * mult).astype(o_ref.dtype)


def _mm_qk(h, w, layer, n, n_scaled, scale, bm=1024, bn=1024):
    s, k = h.shape
    return pl.pallas_call(
        functools.partial(_mm_qk_kernel, n_scaled_blocks=n_scaled // bn, scale=scale),
        grid=(s // bm, n // bn),
        in_specs=[pl.BlockSpec((bm, k), lambda i, j: (i, 0)),
                  pl.BlockSpec((None, k, bn), lambda i, j: (layer, 0, j))],
        out_specs=pl.BlockSpec((bm, bn), lambda i, j: (i, j)),
        out_shape=jax.ShapeDtypeStruct((s, n), BF16),
        compiler_params=_params(
            ("parallel", "parallel"), [((bm, k), BF16), ((k, bn), w.dtype), ((bm, bn), BF16)],
            scratch=[((bm, bn), F32)]),
        name="mm_qk",
    )(h, w)


def _mm_nt_kernel(wt_ref, b_ref, h_ref, o_ref):
    acc = lax.dot_general(wt_ref[...], h_ref[...], (((1,), (1,)), ((), ())), preferred_element_type=F32)
    o_ref[0] = (acc + b_ref[...]).astype(o_ref.dtype)


def _mm_nt(h, wt, col_bias, bm, bn):
    s, k = h.shape
    n = wt.shape[0]
    return pl.pallas_call(
        _mm_nt_kernel,
        grid=(s // bm, n // bn),
        in_specs=[
            pl.BlockSpec((bn, k), lambda i, j: (j, 0)),
            pl.BlockSpec((bn, 1), lambda i, j: (j, 0)),
            pl.BlockSpec((bm, k), lambda i, j: (i, 0)),
        ],
        out_specs=pl.BlockSpec((1, bn, bm), lambda i, j: (i, j, 0)),
        out_shape=jax.ShapeDtypeStruct((s // bm, n, bm), BF16),
        compiler_params=_params(
            ("parallel", "parallel"),
            [((bn, k), BF16), ((bn, V7X_LANES), F32), ((bm, k), BF16), ((bn, bm), BF16)],
            scratch=[((bn, bm), F32)]),
        name="mm_nt",
    )(wt, col_bias, h)


def _value_weights(w_v, n_heads):
    k, d = w_v.shape
    hd = d // n_heads
    wt = jnp.pad(w_v.T.reshape(n_heads, hd, k), ((0, 0), (0, V_PAD), (0, 0)))
    ones_row = jnp.zeros((n_heads, hd + V_PAD, 1), F32).at[:, hd, :].set(1.0)
    return wt.reshape(n_heads * (hd + V_PAD), k), ones_row.reshape(n_heads * (hd + V_PAD), 1)


def _conv_ln_kernel(cur_ref, halo_ref, w_ref, b_ref, g_ref, beta_ref, o_ref, buf_ref, sh_ref, y_ref, *, taps):
    rows, d = cur_ref.shape
    n_tiles = d // V7X_LANES
    sub = V7X_F32_SUBLANES
    span = sh_ref.shape[2]
    first_block = pl.program_id(0) == 0
    for c in range(n_tiles):
        cs = slice(c * V7X_LANES, (c + 1) * V7X_LANES)
        halo = halo_ref[:, cs]
        buf_ref[c, 0:CONV_HALO, :] = jnp.where(first_block, jnp.zeros_like(halo), halo)
        buf_ref[c, CONV_HALO:CONV_HALO + rows, :] = cur_ref[:, cs]
        for s in range(1, sub):
            sh_ref[s - 1, c] = buf_ref[c, s:s + span, :]

    first = CONV_HALO - (taps - 1)

    def tile_body(c, carry):
        acc = jnp.broadcast_to(b_ref[c], (rows, V7X_LANES))
        for t in range(taps):
            base, s = divmod(first + t, sub)
            base *= sub
            src = buf_ref[c, base:base + rows, :] if s == 0 else sh_ref[s - 1, c, base:base + rows, :]
            acc = acc + w_ref[c, t:t + 1, :] * src
        y_ref[c] = acc
        return carry

    lax.fori_loop(0, n_tiles, tile_body, 0)

    total = y_ref[0]
    for c in range(1, n_tiles):
        total = total + y_ref[c]
    mu = jnp.sum(total, axis=-1, keepdims=True) * (1.0 / d)
    sq = jnp.zeros((rows, V7X_LANES), F32)
    for c in range(n_tiles):
        yc = y_ref[c] - mu
        sq = sq + yc * yc
    inv = lax.rsqrt(jnp.sum(sq, axis=-1, keepdims=True) * (1.0 / d) + EPS)
    for c in range(n_tiles):
        z = (y_ref[c] - mu) * inv * g_ref[c] + beta_ref[c]
        o_ref[:, c * V7X_LANES:(c + 1) * V7X_LANES] = (z * jax.nn.sigmoid(z)).astype(o_ref.dtype)


def _conv_ln(u, dw_w, dw_b, ln_g, ln_b):
    s, d = u.shape
    taps = dw_w.shape[0]
    rows = CONV_ROWS
    per = rows // CONV_HALO
    n_tiles = d // V7X_LANES
    vec = lambda v: v.reshape(n_tiles, 1, V7X_LANES)
    w_tiles = dw_w.reshape(taps, n_tiles, V7X_LANES).transpose(1, 0, 2)
    span = rows + ((CONV_HALO - 1) // V7X_F32_SUBLANES) * V7X_F32_SUBLANES
    buf = (n_tiles, rows + CONV_HALO, V7X_LANES)
    shifted = (V7X_F32_SUBLANES - 1, n_tiles, span, V7X_LANES)
    conv_out = (n_tiles, rows, V7X_LANES)
    vec_spec = pl.BlockSpec((n_tiles, 1, V7X_LANES), lambda i: (0, 0, 0))
    return pl.pallas_call(
        functools.partial(_conv_ln_kernel, taps=taps),
        grid=(s // rows,),
        in_specs=[
            pl.BlockSpec((rows, d), lambda i: (i, 0)),
            pl.BlockSpec((CONV_HALO, d), lambda i: (jnp.maximum(i * per - 1, 0), 0)),
            pl.BlockSpec((n_tiles, taps, V7X_LANES), lambda i: (0, 0, 0)),
            vec_spec, vec_spec, vec_spec,
        ],
        out_specs=pl.BlockSpec((rows, d), lambda i: (i, 0)),
        out_shape=jax.ShapeDtypeStruct((s, d), BF16),
        scratch_shapes=[pltpu.VMEM(buf, F32), pltpu.VMEM(shifted, F32), pltpu.VMEM(conv_out, F32)],
        compiler_params=_params(
            ("parallel",), [((rows, d), F32), ((CONV_HALO, d), F32), ((taps, d), F32), ((rows, d), BF16)],
            scratch=[(buf, F32), (shifted, F32), (conv_out, F32)]),
        name="conv_ln",
    )(u, u, w_tiles, vec(dw_b), vec(ln_g), vec(ln_b))


def _ffn_up_kernel(h_ref, wg_ref, wv_ref, cwg_ref, cwv_ref, cbg_ref, cbv_ref, o_ref,
                   wgb_ref, wvb_ref, buf_ref):
    rows = h_ref.shape[0]
    c = FFN_CARRY

    @pl.when(pl.program_id(1) == 0)
    def _():
        wgb_ref[...] = wg_ref[...].astype(BF16)
        wvb_ref[...] = wv_ref[...].astype(BF16)
        for part in range(2):
            buf_ref[part, 0:c, :] = jnp.zeros((c, buf_ref.shape[2]), F32)

    h = h_ref[...]
    buf_ref[0, c:c + rows, :] = jnp.dot(h, wgb_ref[...], preferred_element_type=F32)
    buf_ref[1, c:c + rows, :] = jnp.dot(h, wvb_ref[...], preferred_element_type=F32)

    def conv(part, cw_ref, cb_ref):
        return (cw_ref[0:1, :] * buf_ref[part, c - 2:c - 2 + rows, :]
                + cw_ref[1:2, :] * buf_ref[part, c - 1:c - 1 + rows, :]
                + cw_ref[2:3, :] * buf_ref[part, c:c + rows, :]
                + cb_ref[...])

    gate = conv(0, cwg_ref, cbg_ref)
    val = conv(1, cwv_ref, cbv_ref)
    o_ref[...] = (gate * jax.nn.sigmoid(gate) * val).astype(o_ref.dtype)
    for part in range(2):
        buf_ref[part, 0:c, :] = buf_ref[part, rows:rows + c, :]


def _ffn_up(h, w_up, layer, dw_w, dw_b, bm=1024, bn=512):
    s, k = h.shape
    f = w_up.shape[2] // 2
    nb = f // bn
    taps = dw_w.shape[0]
    assert taps - 1 <= FFN_CARRY
    dw_b = dw_b.reshape(1, 2 * f)
    buf = (2, bm + FFN_CARRY, bn)
    return pl.pallas_call(
        _ffn_up_kernel,
        grid=(nb, s // bm),
        in_specs=[
            pl.BlockSpec((bm, k), lambda j, i: (i, 0)),
            pl.BlockSpec((None, k, bn), lambda j, i: (layer, 0, j)),
            pl.BlockSpec((None, k, bn), lambda j, i: (layer, 0, j + nb)),
            pl.BlockSpec((taps, bn), lambda j, i: (0, j)),
            pl.BlockSpec((taps, bn), lambda j, i: (0, j + nb)),
            pl.BlockSpec((1, bn), lambda j, i: (0, j)),
            pl.BlockSpec((1, bn), lambda j, i: (0, j + nb)),
        ],
        out_specs=pl.BlockSpec((bm, bn), lambda j, i: (i, j)),
        out_shape=jax.ShapeDtypeStruct((s, f), BF16),
        scratch_shapes=[pltpu.VMEM((k, bn), BF16), pltpu.VMEM((k, bn), BF16), pltpu.VMEM(buf, F32)],
        compiler_params=_params(
            ("arbitrary", "arbitrary"),
            [((bm, k), BF16), ((k, bn), w_up.dtype), ((k, bn), w_up.dtype), ((bm, bn), BF16)],
            scratch=[((k, bn), BF16), ((k, bn), BF16), (buf, F32)]),
        name="ffn_up",
    )(h, w_up, w_up, dw_w, dw_w, dw_b, dw_b)


def _t5_bucket(rel):
    half = NUM_BUCKETS // 2
    max_exact = half // 2
    ret = jnp.where(rel > 0, half, 0)
    n = jnp.abs(rel)
    nf = jnp.maximum(n, 1).astype(jnp.float32)
    large = max_exact + (jnp.log(nf / max_exact) / math.log(MAX_DISTANCE / max_exact)
                         * (half - max_exact)).astype(jnp.int32)
    large = jnp.minimum(large, half - 1)
    return ret + jnp.where(n < max_exact, n, large)


def _far_bucket(block, seq):
    n = np.arange(block + 1, seq + 1, dtype=np.float64)
    half = NUM_BUCKETS // 2
    max_exact = half // 2
    large = max_exact + (np.log(n / max_exact) / math.log(MAX_DISTANCE / max_exact) * (half - max_exact)).astype(np.int64)
    assert large.min() > half - 1, "attention block too small for a constant far-key bias"
    return half - 1


def _bias_tiles_kernel(tbl_ref, bucket_ref, allowed_ref, o_ref, *, far_bucket):
    h = pl.program_id(0)
    far = tbl_ref[h, far_bucket]
    for t in range(2):
        bucket = bucket_ref[t]
        acc = jnp.zeros(bucket.shape, F32)
        for b in range(NUM_BUCKETS if t == 0 else NUM_BUCKETS // 2):
            acc = jnp.where(bucket == b, (tbl_ref[h, b] - far) * LOG2E, acc)
        o_ref[0, t] = jnp.where(allowed_ref[t] != 0, acc, -jnp.inf)


def _bias_tiles(rel_bias, block, seq):
    n_heads = rel_bias.shape[1]
    key = jnp.arange(block, dtype=jnp.int32)[:, None]
    qry = jnp.arange(block, dtype=jnp.int32)[None, :]
    rel = jnp.stack([key - qry, key - qry - block])
    bucket = _t5_bucket(rel)
    allowed = jnp.stack([(key // CHUNK) <= (qry // CHUNK), jnp.ones((block, block), bool)]).astype(jnp.int32)
    return pl.pallas_call(
        functools.partial(_bias_tiles_kernel, far_bucket=_far_bucket(block, seq)),
        grid=(n_heads,),
        in_specs=[
            pl.BlockSpec(memory_space=pltpu.SMEM),
            pl.BlockSpec((2, block, block), lambda h: (0, 0, 0)),
            pl.BlockSpec((2, block, block), lambda h: (0, 0, 0)),
        ],
        out_specs=pl.BlockSpec((1, 2, block, block), lambda h: (h, 0, 0, 0)),
        out_shape=jax.ShapeDtypeStruct((n_heads, 2, block, block), F32),
        compiler_params=_params(
            ("parallel",),
            [((2, block, block), jnp.int32), ((2, block, block), jnp.int32), ((2, block, block), F32)]),
        name="bias_tiles",
    )(rel_bias.T, bucket, allowed)


def _attn_kernel(q_ref, k_ref, vt_ref, bias_ref, lam_ref, g_ref, o_ref, qp_ref, s_ref, m_ref, acc_ref, *,
                 lambda_init):
    blk, hd = q_ref.shape
    qi = pl.program_id(1)
    q = q_ref[...]
    lane = lax.broadcasted_iota(jnp.int32, q.shape, 1)
    zero = jnp.zeros_like(q)
    qp_ref[0] = jnp.where(lane < HEAD_DIM, q, zero)
    qp_ref[1] = jnp.where(lane >= HEAD_DIM, q, zero)
    m_ref[...] = jnp.full(m_ref.shape, -jnp.inf, F32)
    acc_ref[...] = jnp.zeros(acc_ref.shape, F32)

    def scores(c, kb):
        k = k_ref[pl.ds(pl.multiple_of(kb * blk, blk), blk), :]
        s_ref[c] = lax.dot_general(k, qp_ref[c], (((1,), (1,)), ((), ())), preferred_element_type=F32)

    def accumulate(c, kb, bias_tile):
        vt = vt_ref[kb]
        for j in range(blk // ATTN_QCHUNK):
            cs = pl.ds(j * ATTN_QCHUNK, ATTN_QCHUNK)
            s = s_ref[c, :, cs]
            if bias_tile is not None:
                s = s + bias_ref[0, bias_tile, :, cs]
            m_old = m_ref[c, :, cs]
            m_new = jnp.maximum(m_old, jnp.max(s, axis=0, keepdims=True))
            alpha = jnp.exp2(m_old - m_new)
            p = jnp.exp2(s - m_new).astype(vt.dtype)
            acc_ref[c, :, cs] = alpha * acc_ref[c, :, cs] + jnp.dot(vt, p, preferred_element_type=F32)
            m_ref[c, :, cs] = m_new

    def block_step(kb, bias_tile, has_next):
        scores(1, kb)
        accumulate(0, kb, bias_tile)
        if has_next:
            scores(0, kb + 1)
        accumulate(1, kb, bias_tile)

    scores(0, 0)

    def far_blocks(first, count):
        for u in range(count):
            block_step(first + u, None, True)

    def far_group(group, carry):
        far_blocks(ATTN_UNROLL * group, ATTN_UNROLL)
        return carry

    n_far = jnp.maximum(qi - 1, 0)
    n_grouped = (n_far // ATTN_UNROLL) * ATTN_UNROLL
    lax.fori_loop(0, n_far // ATTN_UNROLL, far_group, 0)
    rest = n_far - n_grouped
    size = ATTN_UNROLL // 2
    while size >= 1:
        first = n_grouped + (rest // (2 * size)) * (2 * size)
        pl.when((rest // size) % 2 == 1)(functools.partial(far_blocks, first, size))
        size //= 2

    @pl.when(qi >= 1)
    def _():
        block_step(qi - 1, 1, True)
        block_step(qi, 0, False)

    @pl.when(qi == 0)
    def _():
        block_step(qi, 0, False)

    lam_v = lam_ref[...]
    lam = (jnp.exp(jnp.sum(lam_v[0:1] * lam_v[1:2], axis=-1, keepdims=True))
           - jnp.exp(jnp.sum(lam_v[2:3] * lam_v[3:4], axis=-1, keepdims=True)) + lambda_init)
    o = (acc_ref[0, 0:hd, :] / acc_ref[0, hd:hd + 1, :]
         - lam * (acc_ref[1, 0:hd, :] / acc_ref[1, hd:hd + 1, :]))
    ms = jnp.mean(o * o, axis=0, keepdims=True)
    y = o * lax.rsqrt(ms + EPS) * g_ref[...] * (1.0 - lambda_init)
    o_ref[...] = y.T.astype(o_ref.dtype)


def _attention(qk, vt, bias, lam_vecs, subln_g, lambda_init):
    s, d2 = qk.shape
    d = d2 // 2
    nb, _, blk = vt.shape
    hd = 2 * HEAD_DIM
    hv = hd + V_PAD
    n_heads = d // hd
    return pl.pallas_call(
        functools.partial(_attn_kernel, lambda_init=lambda_init),
        grid=(n_heads, nb),
        in_specs=[
            pl.BlockSpec((blk, hd), lambda h, i: (i, h)),
            pl.BlockSpec((s, hd), lambda h, i: (0, n_heads + h)),
            pl.BlockSpec((nb, hv, blk), lambda h, i: (0, h, 0)),
            pl.BlockSpec((1, 2, blk, blk), lambda h, i: (h, 0, 0, 0)),
            pl.BlockSpec((4, HEAD_DIM), lambda h, i: (0, 0)),
            pl.BlockSpec((hd, 1), lambda h, i: (0, 0)),
        ],
        out_specs=pl.BlockSpec((blk, hd), lambda h, i: (i, h)),
        out_shape=jax.ShapeDtypeStruct((s, d), BF16),
        scratch_shapes=[
            pltpu.VMEM((2, blk, hd), BF16), pltpu.VMEM((2, blk, blk), F32),
            pltpu.VMEM((2, 1, blk), F32), pltpu.VMEM((2, hv, blk), F32)],
        compiler_params=_params(
            ("parallel", "parallel"),
            [((blk, hd), BF16), ((s, hd), BF16), ((nb, hv, blk), BF16), ((2, blk, blk), F32), ((blk, hd), BF16)],
            scratch=[((2, blk, hd), BF16), ((2, blk, blk), F32), ((2, hv, blk), F32)]),
        name="diff_attention",
    )(qk, qk, vt, bias, lam_vecs, subln_g.reshape(hd, 1))


def kernel(x, mix_norm, ffn_norm, conv_w_in, conv_b_in, conv_dw_w, conv_dw_b, conv_ln_g, conv_ln_b, conv_w_out, conv_b_out, attn_w_qkv, attn_lambda_q1, attn_lambda_k1, attn_lambda_q2, attn_lambda_k2, attn_subln_g, attn_w_o, rel_bias, ffn_w_up, ffn_dw_w, ffn_dw_b, ffn_w_down, final_norm_g):
    batch, seq, d = x.shape
    assert batch == 1
    depth = mix_norm.shape[0]
    x = x.reshape(seq, d)
    ffn_w_down = ffn_w_down.astype(BF16)
    conv_w_out = conv_w_out.astype(BF16)
    attn_w_o = attn_w_o.astype(BF16)
    for i in range(depth):
        j = i // N_MIXERS
        if i % N_MIXERS == 0:
            u = _mm_glu(x, mix_norm[i], conv_w_in, j, conv_b_in[j])
            c = _conv_ln(u, conv_dw_w[j], conv_dw_b[j], conv_ln_g[j], conv_ln_b[j])
            x, h = _mm_res_norm(c, conv_w_out, j, x, ffn_norm[i], bias=conv_b_out[j])
        else:
            h = _rmsnorm(x, mix_norm[i], BF16)
            lambda_init = 0.8 - 0.6 * math.exp(-0.3 * i)
            n_heads = d // (2 * HEAD_DIM)
            qk = _mm_qk(h, attn_w_qkv, j, n=2 * d, n_scaled=d, scale=HEAD_DIM ** -0.5 * LOG2E)
            wt_v, ones_col = _value_weights(attn_w_qkv[j, :, 2 * d:].astype(BF16), n_heads)
            vt = _mm_nt(h, wt_v, ones_col, bm=ATTN_BLOCK, bn=wt_v.shape[0] // 2)
            bias = _bias_tiles(rel_bias, ATTN_BLOCK, seq)
            lam_vecs = jnp.stack([attn_lambda_q1[j], attn_lambda_k1[j], attn_lambda_q2[j], attn_lambda_k2[j]])
            o = _attention(qk, vt, bias, lam_vecs, attn_subln_g[j], lambda_init)
            x, h = _mm_res_norm(o, attn_w_o, j, x, ffn_norm[i])
        act = _ffn_up(h, ffn_w_up, i, ffn_dw_w[i], ffn_dw_b[i])
        x = _mm_res(act, ffn_w_down, i, x)
    return _rmsnorm(x, final_norm_g, F32).reshape(batch, seq, d)
```

```python
import functools
import math

import numpy as np
import jax
import jax.numpy as jnp
from jax import lax
from jax.experimental import pallas as pl
from jax.experimental.pallas import tpu as pltpu

F32 = jnp.float32
BF16 = jnp.bfloat16

CHUNK = 64
HEAD_DIM = 64
NUM_BUCKETS = 32
MAX_DISTANCE = 128
EPS = 1e-6
N_MIXERS = 2

V7X_VMEM_BYTES = 64 * 1024 * 1024
V7X_F32_SUBLANES = 8
V7X_LANES = 128
V7X_MXU_WIDTH = 256
VMEM_TEMP_BYTES = 12 * 1024 * 1024

LOG2E = math.log2(math.e)

ATTN_BLOCK = 512
ATTN_UNROLL = 8
ATTN_QCHUNK = 256
V_PAD = 16
CONV_ROWS = 256
CONV_HALO = 32
FFN_CARRY = V7X_F32_SUBLANES


def _nbytes(shape, dtype):
    return int(np.prod(shape)) * jnp.dtype(dtype).itemsize


def _params(semantics, blocks, scratch=()):
    need = 2 * sum(_nbytes(s, d) for s, d in blocks) + sum(_nbytes(s, d) for s, d in scratch)
    limit = min(need + VMEM_TEMP_BYTES, V7X_VMEM_BYTES - 4 * 1024 * 1024)
    return pltpu.CompilerParams(dimension_semantics=semantics, vmem_limit_bytes=limit)


def _mm_glu_kernel(x_ref, ng_ref, wa_ref, wg_ref, ba_ref, bg_ref, o_ref, h_ref):
    @pl.when(pl.program_id(1) == 0)
    def _():
        x = x_ref[...]
        ms = jnp.mean(x * x, axis=-1, keepdims=True)
        h_ref[...] = (x * lax.rsqrt(ms + EPS) * ng_ref[...]).astype(h_ref.dtype)

    h = h_ref[...]
    a = jnp.dot(h, wa_ref[...].astype(BF16), preferred_element_type=F32) + ba_ref[...]
    g = jnp.dot(h, wg_ref[...].astype(BF16), preferred_element_type=F32) + bg_ref[...]
    o_ref[...] = a * jax.nn.sigmoid(g)


def _mm_glu(x, norm_g, w, layer, b, bm=1024, bn=512):
    s, k = x.shape
    n = w.shape[2] // 2
    nb = n // bn
    b = b.reshape(1, 2 * n)
    return pl.pallas_call(
        _mm_glu_kernel,
        grid=(s // bm, nb),
        in_specs=[
            pl.BlockSpec((bm, k), lambda i, j: (i, 0)),
            pl.BlockSpec((1, k), lambda i, j: (0, 0)),
            pl.BlockSpec((None, k, bn), lambda i, j: (layer, 0, j)),
            pl.BlockSpec((None, k, bn), lambda i, j: (layer, 0, j + nb)),
            pl.BlockSpec((1, bn), lambda i, j: (0, j)),
            pl.BlockSpec((1, bn), lambda i, j: (0, j + nb)),
        ],
        out_specs=pl.BlockSpec((bm, bn), lambda i, j: (i, j)),
        out_shape=jax.ShapeDtypeStruct((s, n), F32),
        scratch_shapes=[pltpu.VMEM((bm, k), BF16)],
        compiler_params=_params(
            ("parallel", "arbitrary"),
            [((bm, k), F32), ((k, bn), w.dtype), ((k, bn), w.dtype), ((bm, bn), F32)],
            scratch=[((bm, k), BF16)]),
        name="mm_glu",
    )(x, norm_g.reshape(1, k), w, w, b, b)


def _mm_res_norm_kernel(a_ref, w_ref, *refs, has_bias, emit_x):
    refs = list(refs)
    b_ref = refs.pop(0) if has_bias else None
    r_ref, g_ref = refs.pop(0), refs.pop(0)
    xo_ref = refs.pop(0) if emit_x else None
    ho_ref = refs.pop(0)
    y = jnp.dot(a_ref[...], w_ref[...], preferred_element_type=F32)
    if has_bias:
        y = y + b_ref[...]
    y = r_ref[...] + y
    if emit_x:
        xo_ref[...] = y
    ms = jnp.mean(y * y, axis=-1, keepdims=True)
    ho_ref[...] = (y * lax.rsqrt(ms + EPS) * g_ref[...]).astype(ho_ref.dtype)


def _mm_res_norm(a, w, layer, res, norm_g, bias=None, out_dtype=BF16, emit_x=True):
    s, k = a.shape
    n = w.shape[2]
    budget = V7X_VMEM_BYTES - 8 * 1024 * 1024 - VMEM_TEMP_BYTES
    w_bytes = _nbytes((k, n), BF16)
    w_buffers = 2 if 2 * w_bytes <= budget // 2 else 1
    row_bytes = lambda bm: (_nbytes((bm, k), BF16) + _nbytes((bm, n), F32) * (2 if emit_x else 1)
                            + _nbytes((bm, n), out_dtype))
    bm = next(c for c in (512, 256, 128) if w_buffers * w_bytes + 2 * row_bytes(c) <= budget)
    row_spec = lambda cols: pl.BlockSpec((bm, cols), lambda i: (i, 0))
    vec_spec = pl.BlockSpec((1, n), lambda i: (0, 0))
    w_mode = {} if w_buffers == 2 else {"pipeline_mode": pl.Buffered(1)}
    specs = [row_spec(k), pl.BlockSpec((None, k, n), lambda i: (layer, 0, 0), **w_mode)]
    args = [a, w]
    if bias is not None:
        specs.append(vec_spec)
        args.append(bias.reshape(1, n))
    specs += [row_spec(n), vec_spec]
    args += [res, norm_g.reshape(1, n)]
    outs = [jax.ShapeDtypeStruct((s, n), out_dtype)]
    if emit_x:
        outs.insert(0, jax.ShapeDtypeStruct((s, n), F32))
    limit = w_buffers * w_bytes + 2 * row_bytes(bm) + VMEM_TEMP_BYTES
    out = pl.pallas_call(
        functools.partial(_mm_res_norm_kernel, has_bias=bias is not None, emit_x=emit_x),
        grid=(s // bm,),
        in_specs=specs,
        out_specs=[row_spec(n)] * len(outs),
        out_shape=outs,
        compiler_params=pltpu.CompilerParams(dimension_semantics=("parallel",), vmem_limit_bytes=limit),
        name="mm_res_norm",
    )(*args)
    return (out[0], out[1]) if emit_x else (None, out[0])


def _mm_qk_kernel(h_ref, w_ref, o_ref, *, n_scaled_blocks, scale):
    acc = jnp.dot(h_ref[...], w_ref[...].astype(BF16), preferred_element_type=F32)
    mult = jnp.where(pl.program_id(1) < n_scaled_blocks, scale, 1.0).astype(F32)
    o_ref[...] = (acc * mult).astype(o_ref.dtype)


def _mm_qk(h, w, layer, n, n_scaled, scale, bm=1024, bn=1024):
    s, k = h.shape
    return pl.pallas_call(
        functools.partial(_mm_qk_kernel, n_scaled_blocks=n_scaled // bn, scale=scale),
        grid=(s // bm, n // bn),
        in_specs=[pl.BlockSpec((bm, k), lambda i, j: (i, 0)),
                  pl.BlockSpec((None, k, bn), lambda i, j: (layer, 0, j))],
        out_specs=pl.BlockSpec((bm, bn), lambda i, j: (i, j)),
        out_shape=jax.ShapeDtypeStruct((s, n), BF16),
        compiler_params=_params(
            ("parallel", "parallel"), [((bm, k), BF16), ((k, bn), w.dtype), ((bm, bn), BF16)],
            scratch=[((bm, bn), F32)]),
        name="mm_qk",
    )(h, w)


def _mm_nt_kernel(wt_ref, b_ref, h_ref, o_ref):
    acc = lax.dot_general(wt_ref[...], h_ref[...], (((1,), (1,)), ((), ())), preferred_element_type=F32)
    o_ref[0] = (acc + b_ref[...]).astype(o_ref.dtype)


def _mm_nt(h, wt, col_bias, bm, bn):
    s, k = h.shape
    n = wt.shape[0]
    return pl.pallas_call(
        _mm_nt_kernel,
        grid=(s // bm, n // bn),
        in_specs=[
            pl.BlockSpec((bn, k), lambda i, j: (j, 0)),
            pl.BlockSpec((bn, 1), lambda i, j: (j, 0)),
            pl.BlockSpec((bm, k), lambda i, j: (i, 0)),
        ],
        out_specs=pl.BlockSpec((1, bn, bm), lambda i, j: (i, j, 0)),
        out_shape=jax.ShapeDtypeStruct((s // bm, n, bm), BF16),
        compiler_params=_params(
            ("parallel", "parallel"),
            [((bn, k), BF16), ((bn, V7X_LANES), F32), ((bm, k), BF16), ((bn, bm), BF16)],
            scratch=[((bn, bm), F32)]),
        name="mm_nt",
    )(wt, col_bias, h)


def _value_weights(w_v, n_heads):
    k, d = w_v.shape
    hd = d // n_heads
    wt = jnp.pad(w_v.T.reshape(n_heads, hd, k), ((0, 0), (0, V_PAD), (0, 0)))
    ones_row = jnp.zeros((n_heads, hd + V_PAD, 1), F32).at[:, hd, :].set(1.0)
    return wt.reshape(n_heads * (hd + V_PAD), k), ones_row.reshape(n_heads * (hd + V_PAD), 1)


def _conv_ln_kernel(cur_ref, halo_ref, w_ref, b_ref, g_ref, beta_ref, o_ref, buf_ref, sh_ref, y_ref, *, taps):
    rows, d = cur_ref.shape
    n_tiles = d // V7X_LANES
    sub = V7X_F32_SUBLANES
    span = sh_ref.shape[2]
    first_block = pl.program_id(0) == 0
    for c in range(n_tiles):
        cs = slice(c * V7X_LANES, (c + 1) * V7X_LANES)
        halo = halo_ref[:, cs]
        buf_ref[c, 0:CONV_HALO, :] = jnp.where(first_block, jnp.zeros_like(halo), halo)
        buf_ref[c, CONV_HALO:CONV_HALO + rows, :] = cur_ref[:, cs]
        for s in range(1, sub):
            sh_ref[s - 1, c] = buf_ref[c, s:s + span, :]

    first = CONV_HALO - (taps - 1)

    def tile_body(c, carry):
        acc = jnp.broadcast_to(b_ref[c], (rows, V7X_LANES))
        for t in range(taps):
            base, s = divmod(first + t, sub)
            base *= sub
            src = buf_ref[c, base:base + rows, :] if s == 0 else sh_ref[s - 1, c, base:base + rows, :]
            acc = acc + w_ref[c, t:t + 1, :] * src
        y_ref[c] = acc
        return carry

    lax.fori_loop(0, n_tiles, tile_body, 0)

    total = y_ref[0]
    for c in range(1, n_tiles):
        total = total + y_ref[c]
    mu = jnp.sum(total, axis=-1, keepdims=True) * (1.0 / d)
    sq = jnp.zeros((rows, V7X_LANES), F32)
    for c in range(n_tiles):
        yc = y_ref[c] - mu
        sq = sq + yc * yc
    inv = lax.rsqrt(jnp.sum(sq, axis=-1, keepdims=True) * (1.0 / d) + EPS)
    for c in range(n_tiles):
        z = (y_ref[c] - mu) * inv * g_ref[c] + beta_ref[c]
        o_ref[:, c * V7X_LANES:(c + 1) * V7X_LANES] = (z * jax.nn.sigmoid(z)).astype(o_ref.dtype)


def _conv_ln(u, dw_w, dw_b, ln_g, ln_b):
    s, d = u.shape
    taps = dw_w.shape[0]
    rows = CONV_ROWS
    per = rows // CONV_HALO
    n_tiles = d // V7X_LANES
    vec = lambda v: v.reshape(n_tiles, 1, V7X_LANES)
    w_tiles = dw_w.reshape(taps, n_tiles, V7X_LANES).transpose(1, 0, 2)
    span = rows + ((CONV_HALO - 1) // V7X_F32_SUBLANES) * V7X_F32_SUBLANES
    buf = (n_tiles, rows + CONV_HALO, V7X_LANES)
    shifted = (V7X_F32_SUBLANES - 1, n_tiles, span, V7X_LANES)
    conv_out = (n_tiles, rows, V7X_LANES)
    vec_spec = pl.BlockSpec((n_tiles, 1, V7X_LANES), lambda i: (0, 0, 0))
    return pl.pallas_call(
        functools.partial(_conv_ln_kernel, taps=taps),
        grid=(s // rows,),
        in_specs=[
            pl.BlockSpec((rows, d), lambda i: (i, 0)),
            pl.BlockSpec((CONV_HALO, d), lambda i: (jnp.maximum(i * per - 1, 0), 0)),
            pl.BlockSpec((n_tiles, taps, V7X_LANES), lambda i: (0, 0, 0)),
            vec_spec, vec_spec, vec_spec,
        ],
        out_specs=pl.BlockSpec((rows, d), lambda i: (i, 0)),
        out_shape=jax.ShapeDtypeStruct((s, d), BF16),
        scratch_shapes=[pltpu.VMEM(buf, F32), pltpu.VMEM(shifted, F32), pltpu.VMEM(conv_out, F32)],
        compiler_params=_params(
            ("parallel",), [((rows, d), F32), ((CONV_HALO, d), F32), ((taps, d), F32), ((rows, d), BF16)],
            scratch=[(buf, F32), (shifted, F32), (conv_out, F32)]),
        name="conv_ln",
    )(u, u, w_tiles, vec(dw_b), vec(ln_g), vec(ln_b))


def _ffn_up_kernel(h_ref, wg_ref, wv_ref, cwg_ref, cwv_ref, cbg_ref, cbv_ref, o_ref,
                   wgb_ref, wvb_ref, buf_ref):
    rows = h_ref.shape[0]
    c = FFN_CARRY

    @pl.when(pl.program_id(1) == 0)
    def _():
        wgb_ref[...] = wg_ref[...].astype(BF16)
        wvb_ref[...] = wv_ref[...].astype(BF16)
        for part in range(2):
            buf_ref[part, 0:c, :] = jnp.zeros((c, buf_ref.shape[2]), F32)

    h = h_ref[...]
    buf_ref[0, c:c + rows, :] = jnp.dot(h, wgb_ref[...], preferred_element_type=F32)
    buf_ref[1, c:c + rows, :] = jnp.dot(h, wvb_ref[...], preferred_element_type=F32)

    def conv(part, cw_ref, cb_ref):
        return (cw_ref[0:1, :] * buf_ref[part, c - 2:c - 2 + rows, :]
                + cw_ref[1:2, :] * buf_ref[part, c - 1:c - 1 + rows, :]
                + cw_ref[2:3, :] * buf_ref[part, c:c + rows, :]
                + cb_ref[...])

    gate = conv(0, cwg_ref, cbg_ref)
    val = conv(1, cwv_ref, cbv_ref)
    o_ref[...] = (gate * jax.nn.sigmoid(gate) * val).astype(o_ref.dtype)
    for part in range(2):
        buf_ref[part, 0:c, :] = buf_ref[part, rows:rows + c, :]


def _ffn_up(h, w_up, layer, dw_w, dw_b, bm=1024, bn=512):
    s, k = h.shape
    f = w_up.shape[2] // 2
    nb = f // bn
    taps = dw_w.shape[0]
    assert taps - 1 <= FFN_CARRY
    dw_b = dw_b.reshape(1, 2 * f)
    buf = (2, bm + FFN_CARRY, bn)
    return pl.pallas_call(
        _ffn_up_kernel,
        grid=(nb, s // bm),
        in_specs=[
            pl.BlockSpec((bm, k), lambda j, i: (i, 0)),
            pl.BlockSpec((None, k, bn), lambda j, i: (layer, 0, j)),
            pl.BlockSpec((None, k, bn), lambda j, i: (layer, 0, j + nb)),
            pl.BlockSpec((taps, bn), lambda j, i: (0, j)),
            pl.BlockSpec((taps, bn), lambda j, i: (0, j + nb)),
            pl.BlockSpec((1, bn), lambda j, i: (0, j)),
            pl.BlockSpec((1, bn), lambda j, i: (0, j + nb)),
        ],
        out_specs=pl.BlockSpec((bm, bn), lambda j, i: (i, j)),
        out_shape=jax.ShapeDtypeStruct((s, f), BF16),
        scratch_shapes=[pltpu.VMEM((k, bn), BF16), pltpu.VMEM((k, bn), BF16), pltpu.VMEM(buf, F32)],
        compiler_params=_params(
            ("arbitrary", "arbitrary"),
            [((bm, k), BF16), ((k, bn), w_up.dtype), ((k, bn), w_up.dtype), ((bm, bn), BF16)],
            scratch=[((k, bn), BF16), ((k, bn), BF16), (buf, F32)]),
        name="ffn_up",
    )(h, w_up, w_up, dw_w, dw_w, dw_b, dw_b)


def _t5_bucket(rel):
    half = NUM_BUCKETS // 2
    max_exact = half // 2
    ret = jnp.where(rel > 0, half, 0)
    n = jnp.abs(rel)
    nf = jnp.maximum(n, 1).astype(jnp.float32)
    large = max_exact + (jnp.log(nf / max_exact) / math.log(MAX_DISTANCE / max_exact)
                         * (half - max_exact)).astype(jnp.int32)
    large = jnp.minimum(large, half - 1)
    return ret + jnp.where(n < max_exact, n, large)


def _far_bucket(block, seq):
    n = np.arange(block + 1, seq + 1, dtype=np.float64)
    half = NUM_BUCKETS // 2
    max_exact = half // 2
    large = max_exact + (np.log(n / max_exact) / math.log(MAX_DISTANCE / max_exact) * (half - max_exact)).astype(np.int64)
    assert large.min() > half - 1, "attention block too small for a constant far-key bias"
    return half - 1


def _bias_tiles_kernel(tbl_ref, bucket_ref, allowed_ref, o_ref, *, far_bucket):
    h = pl.program_id(0)
    far = tbl_ref[h, far_bucket]
    for t in range(2):
        bucket = bucket_ref[t]
        acc = jnp.zeros(bucket.shape, F32)
        for b in range(NUM_BUCKETS if t == 0 else NUM_BUCKETS // 2):
            acc = jnp.where(bucket == b, (tbl_ref[h, b] - far) * LOG2E, acc)
        o_ref[0, t] = jnp.where(allowed_ref[t] != 0, acc, -jnp.inf)


def _bias_tiles(rel_bias, block, seq):
    n_heads = rel_bias.shape[1]
    key = jnp.arange(block, dtype=jnp.int32)[:, None]
    qry = jnp.arange(block, dtype=jnp.int32)[None, :]
    rel = jnp.stack([key - qry, key - qry - block])
    bucket = _t5_bucket(rel)
    allowed = jnp.stack([(key // CHUNK) <= (qry // CHUNK), jnp.ones((block, block), bool)]).astype(jnp.int32)
    return pl.pallas_call(
        functools.partial(_bias_tiles_kernel, far_bucket=_far_bucket(block, seq)),
        grid=(n_heads,),
        in_specs=[
            pl.BlockSpec(memory_space=pltpu.SMEM),
            pl.BlockSpec((2, block, block), lambda h: (0, 0, 0)),
            pl.BlockSpec((2, block, block), lambda h: (0, 0, 0)),
        ],
        out_specs=pl.BlockSpec((1, 2, block, block), lambda h: (h, 0, 0, 0)),
        out_shape=jax.ShapeDtypeStruct((n_heads, 2, block, block), F32),
        compiler_params=_params(
            ("parallel",),
            [((2, block, block), jnp.int32), ((2, block, block), jnp.int32), ((2, block, block), F32)]),
        name="bias_tiles",
    )(rel_bias.T, bucket, allowed)


def _attn_kernel(q_ref, k_ref, vt_ref, bias_ref, lam_ref, g_ref, o_ref, qp_ref, s_ref, m_ref, acc_ref, *,
                 lambda_init):
    blk, hd = q_ref.shape
    qi = pl.program_id(1)
    q = q_ref[...]
    lane = lax.broadcasted_iota(jnp.int32, q.shape, 1)
    zero = jnp.zeros_like(q)
    qp_ref[0] = jnp.where(lane < HEAD_DIM, q, zero)
    qp_ref[1] = jnp.where(lane >= HEAD_DIM, q, zero)
    m_ref[...] = jnp.full(m_ref.shape, -jnp.inf, F32)
    acc_ref[...] = jnp.zeros(acc_ref.shape, F32)

    def scores(c, kb):
        k = k_ref[pl.ds(pl.multiple_of(kb * blk, blk), blk), :]
        s_ref[c] = lax.dot_general(k, qp_ref[c], (((1,), (1,)), ((), ())), preferred_element_type=F32)

    def accumulate(c, kb, bias_tile):
        vt = vt_ref[kb]
        for j in range(blk // ATTN_QCHUNK):
            cs = pl.ds(j * ATTN_QCHUNK, ATTN_QCHUNK)
            s = s_ref[c, :, cs]
            if bias_tile is not None:
                s = s + bias_ref[0, bias_tile, :, cs]
            m_old = m_ref[c, :, cs]
            m_new = jnp.maximum(m_old, jnp.max(s, axis=0, keepdims=True))
            alpha = jnp.exp2(m_old - m_new)
            p = jnp.exp2(s - m_new).astype(vt.dtype)
            acc_ref[c, :, cs] = alpha * acc_ref[c, :, cs] + jnp.dot(vt, p, preferred_element_type=F32)
            m_ref[c, :, cs] = m_new

    def block_step(kb, bias_tile, has_next):
        scores(1, kb)
        accumulate(0, kb, bias_tile)
        if has_next:
            scores(0, kb + 1)
        accumulate(1, kb, bias_tile)

    scores(0, 0)

    def far_blocks(first, count):
        for u in range(count):
            block_step(first + u, None, True)

    def far_group(group, carry):
        far_blocks(ATTN_UNROLL * group, ATTN_UNROLL)
        return carry

    n_far = jnp.maximum(qi - 1, 0)
    n_grouped = (n_far // ATTN_UNROLL) * ATTN_UNROLL
    lax.fori_loop(0, n_far // ATTN_UNROLL, far_group, 0)
    rest = n_far - n_grouped
    size = ATTN_UNROLL // 2
    while size >= 1:
        first = n_grouped + (rest // (2 * size)) * (2 * size)
        pl.when((rest // size) % 2 == 1)(functools.partial(far_blocks, first, size))
        size //= 2

    @pl.when(qi >= 1)
    def _():
        block_step(qi - 1, 1, True)
        block_step(qi, 0, False)

    @pl.when(qi == 0)
    def _():
        block_step(qi, 0, False)

    lam_v = lam_ref[...]
    lam = (jnp.exp(jnp.sum(lam_v[0:1] * lam_v[1:2], axis=-1, keepdims=True))
           - jnp.exp(jnp.sum(lam_v[2:3] * lam_v[3:4], axis=-1, keepdims=True)) + lambda_init)
    o = (acc_ref[0, 0:hd, :] / acc_ref[0, hd:hd + 1, :]
         - lam * (acc_ref[1, 0:hd, :] / acc_ref[1, hd:hd + 1, :]))
    ms = jnp.mean(o * o, axis=0, keepdims=True)
    y = o * lax.rsqrt(ms + EPS) * g_ref[...] * (1.0 - lambda_init)
    o_ref[...] = y.T.astype(o_ref.dtype)


def _attention(qk, vt, bias, lam_vecs, subln_g, lambda_init):
    s, d2 = qk.shape
    d = d2 // 2
    nb, _, blk = vt.shape
    hd = 2 * HEAD_DIM
    hv = hd + V_PAD
    n_heads = d // hd
    return pl.pallas_call(
        functools.partial(_attn_kernel, lambda_init=lambda_init),
        grid=(n_heads, nb),
        in_specs=[
            pl.BlockSpec((blk, hd), lambda h, i: (i, h)),
            pl.BlockSpec((s, hd), lambda h, i: (0, n_heads + h)),
            pl.BlockSpec((nb, hv, blk), lambda h, i: (0, h, 0)),
            pl.BlockSpec((1, 2, blk, blk), lambda h, i: (h, 0, 0, 0)),
            pl.BlockSpec((4, HEAD_DIM), lambda h, i: (0, 0)),
            pl.BlockSpec((hd, 1), lambda h, i: (0, 0)),
        ],
        out_specs=pl.BlockSpec((blk, hd), lambda h, i: (i, h)),
        out_shape=jax.ShapeDtypeStruct((s, d), BF16),
        scratch_shapes=[
            pltpu.VMEM((2, blk, hd), BF16), pltpu.VMEM((2, blk, blk), F32),
            pltpu.VMEM((2, 1, blk), F32), pltpu.VMEM((2, hv, blk), F32)],
        compiler_params=_params(
            ("parallel", "parallel"),
            [((blk, hd), BF16), ((s, hd), BF16), ((nb, hv, blk), BF16), ((2, blk, blk), F32), ((blk, hd), BF16)],
            scratch=[((2, blk, hd), BF16), ((2, blk, blk), F32), ((2, hv, blk), F32)]),
        name="diff_attention",
    )(qk, qk, vt, bias, lam_vecs, subln_g.reshape(hd, 1))


def kernel(x, mix_norm, ffn_norm, conv_w_in, conv_b_in, conv_dw_w, conv_dw_b, conv_ln_g, conv_ln_b, conv_w_out, conv_b_out, attn_w_qkv, attn_lambda_q1, attn_lambda_k1, attn_lambda_q2, attn_lambda_k2, attn_subln_g, attn_w_o, rel_bias, ffn_w_up, ffn_dw_w, ffn_dw_b, ffn_w_down, final_norm_g):
    batch, seq, d = x.shape
    assert batch == 1
    depth = mix_norm.shape[0]
    x = x.reshape(seq, d)
    ffn_w_down = ffn_w_down.astype(BF16)
    conv_w_out = conv_w_out.astype(BF16)
    attn_w_o = attn_w_o.astype(BF16)
    h = None
    for i in range(depth):
        j = i // N_MIXERS
        if i % N_MIXERS == 0:
            u = _mm_glu(x, mix_norm[i], conv_w_in, j, conv_b_in[j])
            c = _conv_ln(u, conv_dw_w[j], conv_dw_b[j], conv_ln_g[j], conv_ln_b[j])
            x, h = _mm_res_norm(c, conv_w_out, j, x, ffn_norm[i], bias=conv_b_out[j])
        else:
            lambda_init = 0.8 - 0.6 * math.exp(-0.3 * i)
            n_heads = d // (2 * HEAD_DIM)
            qk = _mm_qk(h, attn_w_qkv, j, n=2 * d, n_scaled=d, scale=HEAD_DIM ** -0.5 * LOG2E)
            wt_v, ones_col = _value_weights(attn_w_qkv[j, :, 2 * d:].astype(BF16), n_heads)
            vt = _mm_nt(h, wt_v, ones_col, bm=ATTN_BLOCK, bn=wt_v.shape[0] // 2)
            bias = _bias_tiles(rel_bias, ATTN_BLOCK, seq)
            lam_vecs = jnp.stack([attn_lambda_q1[j], attn_lambda_k1[j], attn_lambda_q2[j], attn_lambda_k2[j]])
            o = _attention(qk, vt, bias, lam_vecs, attn_subln_g[j], lambda_init)
            x, h = _mm_res_norm(o, attn_w_o, j, x, ffn_norm[i])
        act = _ffn_up(h, ffn_w_up, i, ffn_dw_w[i], ffn_dw_b[i])
        last = i == depth - 1
        x, h = _mm_res_norm(act, ffn_w_down, i, x, final_norm_g if last else mix_norm[i + 1],
                            out_dtype=F32 if last else BF16, emit_x=not last)
    return h.reshape(batch, seq, d)
```

```python
import functools
import math

import numpy as np
import jax
import jax.numpy as jnp
from jax import lax
from jax.experimental import pallas as pl
from jax.experimental.pallas import tpu as pltpu

F32 = jnp.float32
BF16 = jnp.bfloat16

CHUNK = 64
HEAD_DIM = 64
NUM_BUCKETS = 32
MAX_DISTANCE = 128
EPS = 1e-6
N_MIXERS = 2

V7X_VMEM_BYTES = 64 * 1024 * 1024
V7X_F32_SUBLANES = 8
V7X_LANES = 128
V7X_MXU_WIDTH = 256
VMEM_TEMP_BYTES = 12 * 1024 * 1024

LOG2E = math.log2(math.e)

ATTN_BLOCK = 512
ATTN_UNROLL = 8
ATTN_QCHUNK = 256
V_PAD = 16
CONV_ROWS = 256
CONV_HALO = 32
FFN_CARRY = V7X_F32_SUBLANES


def _nbytes(shape, dtype):
    return int(np.prod(shape)) * jnp.dtype(dtype).itemsize


def _params(semantics, blocks, scratch=()):
    need = 2 * sum(_nbytes(s, d) for s, d in blocks) + sum(_nbytes(s, d) for s, d in scratch)
    limit = min(need + VMEM_TEMP_BYTES, V7X_VMEM_BYTES - 4 * 1024 * 1024)
    return pltpu.CompilerParams(dimension_semantics=semantics, vmem_limit_bytes=limit)


def _mm_glu_kernel(x_ref, ng_ref, wa_ref, wg_ref, ba_ref, bg_ref, o_ref, h_ref):
    @pl.when(pl.program_id(1) == 0)
    def _():
        x = x_ref[...]
        ms = jnp.mean(x * x, axis=-1, keepdims=True)
        h_ref[...] = (x * lax.rsqrt(ms + EPS) * ng_ref[...]).astype(h_ref.dtype)

    h = h_ref[...]
    a = jnp.dot(h, wa_ref[...].astype(BF16), preferred_element_type=F32) + ba_ref[...]
    g = jnp.dot(h, wg_ref[...].astype(BF16), preferred_element_type=F32) + bg_ref[...]
    o_ref[...] = a * jax.nn.sigmoid(g)


def _mm_glu(x, norm_g, w, layer, b, bm=1024, bn=512):
    s, k = x.shape
    n = w.shape[2] // 2
    nb = n // bn
    b = b.reshape(1, 2 * n)
    return pl.pallas_call(
        _mm_glu_kernel,
        grid=(s // bm, nb),
        in_specs=[
            pl.BlockSpec((bm, k), lambda i, j: (i, 0)),
            pl.BlockSpec((1, k), lambda i, j: (0, 0)),
            pl.BlockSpec((None, k, bn), lambda i, j: (layer, 0, j)),
            pl.BlockSpec((None, k, bn), lambda i, j: (layer, 0, j + nb)),
            pl.BlockSpec((1, bn), lambda i, j: (0, j)),
            pl.BlockSpec((1, bn), lambda i, j: (0, j + nb)),
        ],
        out_specs=pl.BlockSpec((bm, bn), lambda i, j: (i, j)),
        out_shape=jax.ShapeDtypeStruct((s, n), F32),
        scratch_shapes=[pltpu.VMEM((bm, k), BF16)],
        compiler_params=_params(
            ("parallel", "arbitrary"),
            [((bm, k), F32), ((k, bn), w.dtype), ((k, bn), w.dtype), ((bm, bn), F32)],
            scratch=[((bm, k), BF16)]),
        name="mm_glu",
    )(x, norm_g.reshape(1, k), w, w, b, b)


def _mm_res_norm_kernel(a_ref, w_ref, *refs, has_bias, emit_x):
    refs = list(refs)
    b_ref = refs.pop(0) if has_bias else None
    r_ref, g_ref = refs.pop(0), refs.pop(0)
    xo_ref = refs.pop(0) if emit_x else None
    ho_ref = refs.pop(0)
    y = jnp.dot(a_ref[...], w_ref[...], preferred_element_type=F32)
    if has_bias:
        y = y + b_ref[...]
    y = r_ref[...] + y
    if emit_x:
        xo_ref[...] = y
    ms = jnp.mean(y * y, axis=-1, keepdims=True)
    ho_ref[...] = (y * lax.rsqrt(ms + EPS) * g_ref[...]).astype(ho_ref.dtype)


def _mm_res_norm(a, w, layer, res, norm_g, bias=None, out_dtype=BF16, emit_x=True):
    s, k = a.shape
    n = w.shape[2]
    budget = V7X_VMEM_BYTES - 8 * 1024 * 1024 - VMEM_TEMP_BYTES
    w_bytes = _nbytes((k, n), BF16)
    w_buffers = 2 if 2 * w_bytes <= budget // 2 else 1
    row_bytes = lambda bm: (_nbytes((bm, k), BF16) + _nbytes((bm, n), F32) * (2 if emit_x else 1)
                            + _nbytes((bm, n), out_dtype))
    bm = next(c for c in (512, 256, 128) if w_buffers * w_bytes + 2 * row_bytes(c) <= budget)
    row_spec = lambda cols: pl.BlockSpec((bm, cols), lambda i: (i, 0))
    vec_spec = pl.BlockSpec((1, n), lambda i: (0, 0))
    w_mode = {} if w_buffers == 2 else {"pipeline_mode": pl.Buffered(1)}
    specs = [row_spec(k), pl.BlockSpec((None, k, n), lambda i: (layer, 0, 0), **w_mode)]
    args = [a, w]
    if bias is not None:
        specs.append(vec_spec)
        args.append(bias.reshape(1, n))
    specs += [row_spec(n), vec_spec]
    args += [res, norm_g.reshape(1, n)]
    outs = [jax.ShapeDtypeStruct((s, n), out_dtype)]
    if emit_x:
        outs.insert(0, jax.ShapeDtypeStruct((s, n), F32))
    limit = w_buffers * w_bytes + 2 * row_bytes(bm) + VMEM_TEMP_BYTES
    out = pl.pallas_call(
        functools.partial(_mm_res_norm_kernel, has_bias=bias is not None, emit_x=emit_x),
        grid=(s // bm,),
        in_specs=specs,
        out_specs=[row_spec(n)] * len(outs),
        out_shape=outs,
        compiler_params=pltpu.CompilerParams(dimension_semantics=("parallel",), vmem_limit_bytes=limit),
        name="mm_res_norm",
    )(*args)
    return (out[0], out[1]) if emit_x else (None, out[0])


def _mm_qk_kernel(h_ref, w_ref, o_ref, *, n_scaled_blocks, scale):
    acc = jnp.dot(h_ref[...], w_ref[...].astype(BF16), preferred_element_type=F32)
    mult = jnp.where(pl.program_id(1) < n_scaled_blocks, scale, 1.0).astype(F32)
    o_ref[...] = (acc * mult).astype(o_ref.dtype)


def _mm_qk(h, w, layer, n, n_scaled, scale, bm=1024, bn=1024):
    s, k = h.shape
    return pl.pallas_call(
        functools.partial(_mm_qk_kernel, n_scaled_blocks=n_scaled // bn, scale=scale),
        grid=(s // bm, n // bn),
        in_specs=[pl.BlockSpec((bm, k), lambda i, j: (i, 0)),
                  pl.BlockSpec((None, k, bn), lambda i, j: (layer, 0, j))],
        out_specs=pl.BlockSpec((bm, bn), lambda i, j: (i, j)),
        out_shape=jax.ShapeDtypeStruct((s, n), BF16),
        compiler_params=_params(
            ("parallel", "parallel"), [((bm, k), BF16), ((k, bn), w.dtype), ((bm, bn), BF16)],
            scratch=[((bm, bn), F32)]),
        name="mm_qk",
    )(h, w)


def _mm_nt_kernel(wt_ref, b_ref, h_ref, o_ref):
    acc = lax.dot_general(wt_ref[...], h_ref[...], (((1,), (1,)), ((), ())), preferred_element_type=F32)
    o_ref[0] = (acc + b_ref[...]).astype(o_ref.dtype)


def _mm_nt(h, wt, col_bias, bm, bn):
    s, k = h.shape
    n = wt.shape[0]
    return pl.pallas_call(
        _mm_nt_kernel,
        grid=(s // bm, n // bn),
        in_specs=[
            pl.BlockSpec((bn, k), lambda i, j: (j, 0)),
            pl.BlockSpec((bn, 1), lambda i, j: (j, 0)),
            pl.BlockSpec((bm, k), lambda i, j: (i, 0)),
        ],
        out_specs=pl.BlockSpec((1, bn, bm), lambda i, j: (i, j, 0)),
        out_shape=jax.ShapeDtypeStruct((s // bm, n, bm), BF16),
        compiler_params=_params(
            ("parallel", "parallel"),
            [((bn, k), BF16), ((bn, V7X_LANES), F32), ((bm, k), BF16), ((bn, bm), BF16)],
            scratch=[((bn, bm), F32)]),
        name="mm_nt",
    )(wt, col_bias, h)


def _value_weights(w_v, n_heads):
    k, d = w_v.shape
    hd = d // n_heads
    wt = jnp.pad(w_v.T.reshape(n_heads, hd, k), ((0, 0), (0, V_PAD), (0, 0)))
    ones_row = jnp.zeros((n_heads, hd + V_PAD, 1), F32).at[:, hd, :].set(1.0)
    return wt.reshape(n_heads * (hd + V_PAD), k), ones_row.reshape(n_heads * (hd + V_PAD), 1)


def _conv_ln_kernel(cur_ref, halo_ref, w_ref, b_ref, g_ref, beta_ref, o_ref, buf_ref, sh_ref, y_ref, *, taps):
    rows, d = cur_ref.shape
    n_tiles = d // V7X_LANES
    sub = V7X_F32_SUBLANES
    span = sh_ref.shape[2]
    first_block = pl.program_id(0) == 0
    for c in range(n_tiles):
        cs = slice(c * V7X_LANES, (c + 1) * V7X_LANES)
        halo = halo_ref[:, cs]
        buf_ref[c, 0:CONV_HALO, :] = jnp.where(first_block, jnp.zeros_like(halo), halo)
        buf_ref[c, CONV_HALO:CONV_HALO + rows, :] = cur_ref[:, cs]
        for s in range(1, sub):
            sh_ref[s - 1, c] = buf_ref[c, s:s + span, :]

    first = CONV_HALO - (taps - 1)

    def tile_body(c, carry):
        acc = jnp.broadcast_to(b_ref[c], (rows, V7X_LANES))
        for t in range(taps):
            base, s = divmod(first + t, sub)
            base *= sub
            src = buf_ref[c, base:base + rows, :] if s == 0 else sh_ref[s - 1, c, base:base + rows, :]
            acc = acc + w_ref[c, t:t + 1, :] * src
        y_ref[c] = acc
        return carry

    lax.fori_loop(0, n_tiles, tile_body, 0)

    total = y_ref[0]
    for c in range(1, n_tiles):
        total = total + y_ref[c]
    mu = jnp.sum(total, axis=-1, keepdims=True) * (1.0 / d)
    sq = jnp.zeros((rows, V7X_LANES), F32)
    for c in range(n_tiles):
        yc = y_ref[c] - mu
        sq = sq + yc * yc
    inv = lax.rsqrt(jnp.sum(sq, axis=-1, keepdims=True) * (1.0 / d) + EPS)
    for c in range(n_tiles):
        z = (y_ref[c] - mu) * inv * g_ref[c] + beta_ref[c]
        o_ref[:, c * V7X_LANES:(c + 1) * V7X_LANES] = (z * jax.nn.sigmoid(z)).astype(o_ref.dtype)


def _conv_ln(u, dw_w, dw_b, ln_g, ln_b):
    s, d = u.shape
    taps = dw_w.shape[0]
    rows = CONV_ROWS
    per = rows // CONV_HALO
    n_tiles = d // V7X_LANES
    vec = lambda v: v.reshape(n_tiles, 1, V7X_LANES)
    w_tiles = dw_w.reshape(taps, n_tiles, V7X_LANES).transpose(1, 0, 2)
    span = rows + ((CONV_HALO - 1) // V7X_F32_SUBLANES) * V7X_F32_SUBLANES
    buf = (n_tiles, rows + CONV_HALO, V7X_LANES)
    shifted = (V7X_F32_SUBLANES - 1, n_tiles, span, V7X_LANES)
    conv_out = (n_tiles, rows, V7X_LANES)
    vec_spec = pl.BlockSpec((n_tiles, 1, V7X_LANES), lambda i: (0, 0, 0))
    return pl.pallas_call(
        functools.partial(_conv_ln_kernel, taps=taps),
        grid=(s // rows,),
        in_specs=[
            pl.BlockSpec((rows, d), lambda i: (i, 0)),
            pl.BlockSpec((CONV_HALO, d), lambda i: (jnp.maximum(i * per - 1, 0), 0)),
            pl.BlockSpec((n_tiles, taps, V7X_LANES), lambda i: (0, 0, 0)),
            vec_spec, vec_spec, vec_spec,
        ],
        out_specs=pl.BlockSpec((rows, d), lambda i: (i, 0)),
        out_shape=jax.ShapeDtypeStruct((s, d), BF16),
        scratch_shapes=[pltpu.VMEM(buf, F32), pltpu.VMEM(shifted, F32), pltpu.VMEM(conv_out, F32)],
        compiler_params=_params(
            ("parallel",), [((rows, d), F32), ((CONV_HALO, d), F32), ((taps, d), F32), ((rows, d), BF16)],
            scratch=[(buf, F32), (shifted, F32), (conv_out, F32)]),
        name="conv_ln",
    )(u, u, w_tiles, vec(dw_b), vec(ln_g), vec(ln_b))


def _ffn_up_kernel(h_ref, wg_ref, wv_ref, cwg_ref, cwv_ref, cbg_ref, cbv_ref, o_ref,
                   wgb_ref, wvb_ref, buf_ref):
    rows = h_ref.shape[0]
    c = FFN_CARRY

    @pl.when(pl.program_id(1) == 0)
    def _():
        wgb_ref[...] = wg_ref[...].astype(BF16)
        wvb_ref[...] = wv_ref[...].astype(BF16)
        for part in range(2):
            buf_ref[part, 0:c, :] = jnp.zeros((c, buf_ref.shape[2]), F32)

    h = h_ref[...]
    buf_ref[0, c:c + rows, :] = jnp.dot(h, wgb_ref[...], preferred_element_type=F32)
    buf_ref[1, c:c + rows, :] = jnp.dot(h, wvb_ref[...], preferred_element_type=F32)

    def conv(part, cw_ref, cb_ref):
        return (cw_ref[0:1, :] * buf_ref[part, c - 2:c - 2 + rows, :]
                + cw_ref[1:2, :] * buf_ref[part, c - 1:c - 1 + rows, :]
                + cw_ref[2:3, :] * buf_ref[part, c:c + rows, :]
                + cb_ref[...])

    gate = conv(0, cwg_ref, cbg_ref)
    val = conv(1, cwv_ref, cbv_ref)
    o_ref[...] = (gate * jax.nn.sigmoid(gate) * val).astype(o_ref.dtype)
    for part in range(2):
        buf_ref[part, 0:c, :] = buf_ref[part, rows:rows + c, :]


def _ffn_up(h, w_up, layer, dw_w, dw_b, bm=1024, bn=512):
    s, k = h.shape
    f = w_up.shape[2] // 2
    nb = f // bn
    taps = dw_w.shape[0]
    assert taps - 1 <= FFN_CARRY
    dw_b = dw_b.reshape(1, 2 * f)
    buf = (2, bm + FFN_CARRY, bn)
    return pl.pallas_call(
        _ffn_up_kernel,
        grid=(nb, s // bm),
        in_specs=[
            pl.BlockSpec((bm, k), lambda j, i: (i, 0)),
            pl.BlockSpec((None, k, bn), lambda j, i: (layer, 0, j)),
            pl.BlockSpec((None, k, bn), lambda j, i: (layer, 0, j + nb)),
            pl.BlockSpec((taps, bn), lambda j, i: (0, j)),
            pl.BlockSpec((taps, bn), lambda j, i: (0, j + nb)),
            pl.BlockSpec((1, bn), lambda j, i: (0, j)),
            pl.BlockSpec((1, bn), lambda j, i: (0, j + nb)),
        ],
        out_specs=pl.BlockSpec((bm, bn), lambda j, i: (i, j)),
        out_shape=jax.ShapeDtypeStruct((s, f), BF16),
        scratch_shapes=[pltpu.VMEM((k, bn), BF16), pltpu.VMEM((k, bn), BF16), pltpu.VMEM(buf, F32)],
        compiler_params=_params(
            ("arbitrary", "arbitrary"),
            [((bm, k), BF16), ((k, bn), w_up.dtype), ((k, bn), w_up.dtype), ((bm, bn), BF16)],
            scratch=[((k, bn), BF16), ((k, bn), BF16), (buf, F32)]),
        name="ffn_up",
    )(h, w_up, w_up, dw_w, dw_w, dw_b, dw_b)


def _t5_bucket(rel):
    half = NUM_BUCKETS // 2
    max_exact = half // 2
    ret = jnp.where(rel > 0, half, 0)
    n = jnp.abs(rel)
    nf = jnp.maximum(n, 1).astype(jnp.float32)
    large = max_exact + (jnp.log(nf / max_exact) / math.log(MAX_DISTANCE / max_exact)
                         * (half - max_exact)).astype(jnp.int32)
    large = jnp.minimum(large, half - 1)
    return ret + jnp.where(n < max_exact, n, large)


def _far_bucket(block, seq):
    n = np.arange(block + 1, seq + 1, dtype=np.float64)
    half = NUM_BUCKETS // 2
    max_exact = half // 2
    large = max_exact + (np.log(n / max_exact) / math.log(MAX_DISTANCE / max_exact) * (half - max_exact)).astype(np.int64)
    assert large.min() > half - 1, "attention block too small for a constant far-key bias"
    return half - 1


def _bias_tiles_kernel(tbl_ref, bucket_ref, allowed_ref, o_ref, *, far_bucket):
    h = pl.program_id(0)
    far = tbl_ref[h, far_bucket]
    for t in range(2):
        bucket = bucket_ref[t]
        acc = jnp.zeros(bucket.shape, F32)
        for b in range(NUM_BUCKETS if t == 0 else NUM_BUCKETS // 2):
            acc = jnp.where(bucket == b, (tbl_ref[h, b] - far) * LOG2E, acc)
        o_ref[0, t] = jnp.where(allowed_ref[t] != 0, acc, -jnp.inf)


def _bias_tiles(rel_bias, block, seq):
    n_heads = rel_bias.shape[1]
    key = jnp.arange(block, dtype=jnp.int32)[:, None]
    qry = jnp.arange(block, dtype=jnp.int32)[None, :]
    rel = jnp.stack([key - qry, key - qry - block])
    bucket = _t5_bucket(rel)
    allowed = jnp.stack([(key // CHUNK) <= (qry // CHUNK), jnp.ones((block, block), bool)]).astype(jnp.int32)
    return pl.pallas_call(
        functools.partial(_bias_tiles_kernel, far_bucket=_far_bucket(block, seq)),
        grid=(n_heads,),
        in_specs=[
            pl.BlockSpec(memory_space=pltpu.SMEM),
            pl.BlockSpec((2, block, block), lambda h: (0, 0, 0)),
            pl.BlockSpec((2, block, block), lambda h: (0, 0, 0)),
        ],
        out_specs=pl.BlockSpec((1, 2, block, block), lambda h: (h, 0, 0, 0)),
        out_shape=jax.ShapeDtypeStruct((n_heads, 2, block, block), F32),
        compiler_params=_params(
            ("parallel",),
            [((2, block, block), jnp.int32), ((2, block, block), jnp.int32), ((2, block, block), F32)]),
        name="bias_tiles",
    )(rel_bias.T, bucket, allowed)


def _attn_kernel(q_ref, qn_ref, k_ref, vt_ref, bias_ref, lam_ref, g_ref, o_ref, qp_ref, s_ref, m_ref, acc_ref, *,
                 lambda_init):
    blk, hd = q_ref.shape
    qi = pl.program_id(1)
    q = q_ref[...]
    lane = lax.broadcasted_iota(jnp.int32, q.shape, 1)
    zero = jnp.zeros_like(q)
    qp_ref[0] = jnp.where(lane < HEAD_DIM, q, zero)
    qp_ref[1] = jnp.where(lane >= HEAD_DIM, q, zero)
    m_ref[...] = jnp.full(m_ref.shape, -jnp.inf, F32)
    acc_ref[...] = jnp.zeros(acc_ref.shape, F32)

    def scores(c, kb):
        k = k_ref[pl.ds(pl.multiple_of(kb * blk, blk), blk), :]
        s_ref[c] = lax.dot_general(k, qp_ref[c], (((1,), (1,)), ((), ())), preferred_element_type=F32)

    def accumulate(c, kb, bias_tile):
        vt = vt_ref[kb]
        for j in range(blk // ATTN_QCHUNK):
            cs = pl.ds(j * ATTN_QCHUNK, ATTN_QCHUNK)
            s = s_ref[c, :, cs]
            if bias_tile is not None:
                s = s + bias_ref[0, bias_tile, :, cs]
            m_old = m_ref[c, :, cs]
            m_new = jnp.maximum(m_old, jnp.max(s, axis=0, keepdims=True))
            alpha = jnp.exp2(m_old - m_new)
            p = jnp.exp2(s - m_new).astype(vt.dtype)
            acc_ref[c, :, cs] = alpha * acc_ref[c, :, cs] + jnp.dot(vt, p, preferred_element_type=F32)
            m_ref[c, :, cs] = m_new

    def block_step(kb, bias_tile, has_next):
        scores(1, kb)
        accumulate(0, kb, bias_tile)
        if has_next:
            scores(0, kb + 1)
        accumulate(1, kb, bias_tile)

    @pl.when(qi == 0)
    def _():
        scores(0, 0)

    def far_blocks(first, count):
        for u in range(count):
            block_step(first + u, None, True)

    def far_group(group, carry):
        far_blocks(ATTN_UNROLL * group, ATTN_UNROLL)
        return carry

    n_far = jnp.maximum(qi - 1, 0)
    n_grouped = (n_far // ATTN_UNROLL) * ATTN_UNROLL
    lax.fori_loop(0, n_far // ATTN_UNROLL, far_group, 0)
    rest = n_far - n_grouped
    size = ATTN_UNROLL // 2
    while size >= 1:
        first = n_grouped + (rest // (2 * size)) * (2 * size)
        pl.when((rest // size) % 2 == 1)(functools.partial(far_blocks, first, size))
        size //= 2

    @pl.when(qi >= 1)
    def _():
        block_step(qi - 1, 1, True)
        block_step(qi, 0, False)

    @pl.when(qi == 0)
    def _():
        block_step(qi, 0, False)

    q_next = qn_ref[...]
    s_ref[0] = lax.dot_general(k_ref[0:blk, :], jnp.where(lane < HEAD_DIM, q_next, jnp.zeros_like(q_next)),
                               (((1,), (1,)), ((), ())), preferred_element_type=F32)

    lam_v = lam_ref[...]
    lam = (jnp.exp(jnp.sum(lam_v[0:1] * lam_v[1:2], axis=-1, keepdims=True))
           - jnp.exp(jnp.sum(lam_v[2:3] * lam_v[3:4], axis=-1, keepdims=True)) + lambda_init)
    o = (acc_ref[0, 0:hd, :] / acc_ref[0, hd:hd + 1, :]
         - lam * (acc_ref[1, 0:hd, :] / acc_ref[1, hd:hd + 1, :]))
    ms = jnp.mean(o * o, axis=0, keepdims=True)
    y = o * lax.rsqrt(ms + EPS) * g_ref[...] * (1.0 - lambda_init)
    o_ref[...] = y.T.astype(o_ref.dtype)


def _attention(qk, vt, bias, lam_vecs, subln_g, lambda_init):
    s, d2 = qk.shape
    d = d2 // 2
    nb, _, blk = vt.shape
    hd = 2 * HEAD_DIM
    hv = hd + V_PAD
    n_heads = d // hd
    return pl.pallas_call(
        functools.partial(_attn_kernel, lambda_init=lambda_init),
        grid=(n_heads, nb),
        in_specs=[
            pl.BlockSpec((blk, hd), lambda h, i: (i, h)),
            pl.BlockSpec((blk, hd), lambda h, i: (jnp.minimum(i + 1, nb - 1), h)),
            pl.BlockSpec((s, hd), lambda h, i: (0, n_heads + h)),
            pl.BlockSpec((nb, hv, blk), lambda h, i: (0, h, 0)),
            pl.BlockSpec((1, 2, blk, blk), lambda h, i: (h, 0, 0, 0)),
            pl.BlockSpec((4, HEAD_DIM), lambda h, i: (0, 0)),
            pl.BlockSpec((hd, 1), lambda h, i: (0, 0)),
        ],
        out_specs=pl.BlockSpec((blk, hd), lambda h, i: (i, h)),
        out_shape=jax.ShapeDtypeStruct((s, d), BF16),
        scratch_shapes=[
            pltpu.VMEM((2, blk, hd), BF16), pltpu.VMEM((2, blk, blk), F32),
            pltpu.VMEM((2, 1, blk), F32), pltpu.VMEM((2, hv, blk), F32)],
        compiler_params=_params(
            ("arbitrary", "arbitrary"),
            [((blk, hd), BF16), ((blk, hd), BF16), ((s, hd), BF16), ((nb, hv, blk), BF16), ((2, blk, blk), F32),
             ((blk, hd), BF16)],
            scratch=[((2, blk, hd), BF16), ((2, blk, blk), F32), ((2, hv, blk), F32)]),
        name="diff_attention",
    )(qk, qk, qk, vt, bias, lam_vecs, subln_g.reshape(hd, 1))


def kernel(x, mix_norm, ffn_norm, conv_w_in, conv_b_in, conv_dw_w, conv_dw_b, conv_ln_g, conv_ln_b, conv_w_out, conv_b_out, attn_w_qkv, attn_lambda_q1, attn_lambda_k1, attn_lambda_q2, attn_lambda_k2, attn_subln_g, attn_w_o, rel_bias, ffn_w_up, ffn_dw_w, ffn_dw_b, ffn_w_down, final_norm_g):
    batch, seq, d = x.shape
    assert batch == 1
    depth = mix_norm.shape[0]
    x = x.reshape(seq, d)
    ffn_w_down = ffn_w_down.astype(BF16)
    conv_w_out = conv_w_out.astype(BF16)
    attn_w_o = attn_w_o.astype(BF16)
    h = None
    for i in range(depth):
        j = i // N_MIXERS
        if i % N_MIXERS == 0:
            u = _mm_glu(x, mix_norm[i], conv_w_in, j, conv_b_in[j])
            c = _conv_ln(u, conv_dw_w[j], conv_dw_b[j], conv_ln_g[j], conv_ln_b[j])
            x, h = _mm_res_norm(c, conv_w_out, j, x, ffn_norm[i], bias=conv_b_out[j])
        else:
            lambda_init = 0.8 - 0.6 * math.exp(-0.3 * i)
            n_heads = d // (2 * HEAD_DIM)
            qk = _mm_qk(h, attn_w_qkv, j, n=2 * d, n_scaled=d, scale=HEAD_DIM ** -0.5 * LOG2E)
            wt_v, ones_col = _value_weights(attn_w_qkv[j, :, 2 * d:].astype(BF16), n_heads)
            vt = _mm_nt(h, wt_v, ones_col, bm=ATTN_BLOCK, bn=wt_v.shape[0] // 2)
            bias = _bias_tiles(rel_bias, ATTN_BLOCK, seq)
            lam_vecs = jnp.stack([attn_lambda_q1[j], attn_lambda_k1[j], attn_lambda_q2[j], attn_lambda_k2[j]])
            o = _attention(qk, vt, bias, lam_vecs, attn_subln_g[j], lambda_init)
            x, h = _mm_res_norm(o, attn_w_o, j, x, ffn_norm[i])
        act = _ffn_up(h, ffn_w_up, i, ffn_dw_w[i], ffn_dw_b[i])
        last = i == depth - 1
        x, h = _mm_res_norm(act, ffn_w_down, i, x, final_norm_g if last else mix_norm[i + 1],
                            out_dtype=F32 if last else BF16, emit_x=not last)
    return h.reshape(batch, seq, d)
```

```python
import functools
import math

import numpy as np
import jax
import jax.numpy as jnp
from jax import lax
from jax.experimental import pallas as pl
from jax.experimental.pallas import tpu as pltpu

F32 = jnp.float32
BF16 = jnp.bfloat16

CHUNK = 64
HEAD_DIM = 64
NUM_BUCKETS = 32
MAX_DISTANCE = 128
EPS = 1e-6
N_MIXERS = 2

V7X_VMEM_BYTES = 64 * 1024 * 1024
V7X_F32_SUBLANES = 8
V7X_LANES = 128
V7X_MXU_WIDTH = 256
VMEM_TEMP_BYTES = 12 * 1024 * 1024

LOG2E = math.log2(math.e)

ATTN_BLOCK = 512
ATTN_UNROLL = 8
ATTN_QCHUNK = 256
V_PAD = 16
CONV_ROWS = 256
CONV_HALO = 32
FFN_CARRY = V7X_F32_SUBLANES


def _nbytes(shape, dtype):
    return int(np.prod(shape)) * jnp.dtype(dtype).itemsize


def _params(semantics, blocks, scratch=()):
    need = 2 * sum(_nbytes(s, d) for s, d in blocks) + sum(_nbytes(s, d) for s, d in scratch)
    limit = min(need + VMEM_TEMP_BYTES, V7X_VMEM_BYTES - 4 * 1024 * 1024)
    return pltpu.CompilerParams(dimension_semantics=semantics, vmem_limit_bytes=limit)


def _mm_glu_kernel(x_ref, ng_ref, wa_ref, wg_ref, ba_ref, bg_ref, o_ref, h_ref):
    @pl.when(pl.program_id(1) == 0)
    def _():
        x = x_ref[...]
        ms = jnp.mean(x * x, axis=-1, keepdims=True)
        h_ref[...] = (x * lax.rsqrt(ms + EPS) * ng_ref[...]).astype(h_ref.dtype)

    h = h_ref[...]
    a = jnp.dot(h, wa_ref[...].astype(BF16), preferred_element_type=F32) + ba_ref[...]
    g = jnp.dot(h, wg_ref[...].astype(BF16), preferred_element_type=F32) + bg_ref[...]
    o_ref[...] = a * jax.nn.sigmoid(g)


def _mm_glu(x, norm_g, w, layer, b, bm=1024, bn=512):
    s, k = x.shape
    n = w.shape[2] // 2
    nb = n // bn
    b = b.reshape(1, 2 * n)
    return pl.pallas_call(
        _mm_glu_kernel,
        grid=(s // bm, nb),
        in_specs=[
            pl.BlockSpec((bm, k), lambda i, j: (i, 0)),
            pl.BlockSpec((1, k), lambda i, j: (0, 0)),
            pl.BlockSpec((None, k, bn), lambda i, j: (layer, 0, j)),
            pl.BlockSpec((None, k, bn), lambda i, j: (layer, 0, j + nb)),
            pl.BlockSpec((1, bn), lambda i, j: (0, j)),
            pl.BlockSpec((1, bn), lambda i, j: (0, j + nb)),
        ],
        out_specs=pl.BlockSpec((bm, bn), lambda i, j: (i, j)),
        out_shape=jax.ShapeDtypeStruct((s, n), F32),
        scratch_shapes=[pltpu.VMEM((bm, k), BF16)],
        compiler_params=_params(
            ("parallel", "arbitrary"),
            [((bm, k), F32), ((k, bn), w.dtype), ((k, bn), w.dtype), ((bm, bn), F32)],
            scratch=[((bm, k), BF16)]),
        name="mm_glu",
    )(x, norm_g.reshape(1, k), w, w, b, b)


def _mm_res_norm_kernel(a_ref, w_ref, *refs, has_bias, emit_x):
    refs = list(refs)
    b_ref = refs.pop(0) if has_bias else None
    r_ref, g_ref = refs.pop(0), refs.pop(0)
    xo_ref = refs.pop(0) if emit_x else None
    ho_ref = refs.pop(0)
    y = jnp.dot(a_ref[...], w_ref[...], preferred_element_type=F32)
    if has_bias:
        y = y + b_ref[...]
    y = r_ref[...] + y
    if emit_x:
        xo_ref[...] = y
    ms = jnp.mean(y * y, axis=-1, keepdims=True)
    ho_ref[...] = (y * lax.rsqrt(ms + EPS) * g_ref[...]).astype(ho_ref.dtype)


def _mm_res_norm(a, w, layer, res, norm_g, bias=None, out_dtype=BF16, emit_x=True):
    s, k = a.shape
    n = w.shape[2]
    budget = V7X_VMEM_BYTES - 8 * 1024 * 1024 - VMEM_TEMP_BYTES
    w_bytes = _nbytes((k, n), BF16)
    w_buffers = 2 if 2 * w_bytes <= budget // 2 else 1
    row_bytes = lambda bm: (_nbytes((bm, k), BF16) + _nbytes((bm, n), F32) * (2 if emit_x else 1)
                            + _nbytes((bm, n), out_dtype))
    bm = next(c for c in (512, 256, 128) if w_buffers * w_bytes + 2 * row_bytes(c) <= budget)
    row_spec = lambda cols: pl.BlockSpec((bm, cols), lambda i: (i, 0))
    vec_spec = pl.BlockSpec((1, n), lambda i: (0, 0))
    w_mode = {} if w_buffers == 2 else {"pipeline_mode": pl.Buffered(1)}
    specs = [row_spec(k), pl.BlockSpec((None, k, n), lambda i: (layer, 0, 0), **w_mode)]
    args = [a, w]
    if bias is not None:
        specs.append(vec_spec)
        args.append(bias.reshape(1, n))
    specs += [row_spec(n), vec_spec]
    args += [res, norm_g.reshape(1, n)]
    outs = [jax.ShapeDtypeStruct((s, n), out_dtype)]
    if emit_x:
        outs.insert(0, jax.ShapeDtypeStruct((s, n), F32))
    limit = w_buffers * w_bytes + 2 * row_bytes(bm) + VMEM_TEMP_BYTES
    out = pl.pallas_call(
        functools.partial(_mm_res_norm_kernel, has_bias=bias is not None, emit_x=emit_x),
        grid=(s // bm,),
        in_specs=specs,
        out_specs=[row_spec(n)] * len(outs),
        out_shape=outs,
        compiler_params=pltpu.CompilerParams(dimension_semantics=("parallel",), vmem_limit_bytes=limit),
        name="mm_res_norm",
    )(*args)
    return (out[0], out[1]) if emit_x else (None, out[0])


def _mm_qk_kernel(h_ref, w_ref, o_ref, *, n_scaled_blocks, scale):
    acc = jnp.dot(h_ref[...], w_ref[...].astype(BF16), preferred_element_type=F32)
    mult = jnp.where(pl.program_id(1) < n_scaled_blocks, scale, 1.0).astype(F32)
    o_ref[...] = (acc * mult).astype(o_ref.dtype)


def _mm_qk(h, w, layer, n, n_scaled, scale, bm=1024, bn=1024):
    s, k = h.shape
    return pl.pallas_call(
        functools.partial(_mm_qk_kernel, n_scaled_blocks=n_scaled // bn, scale=scale),
        grid=(s // bm, n // bn),
        in_specs=[pl.BlockSpec((bm, k), lambda i, j: (i, 0)),
                  pl.BlockSpec((None, k, bn), lambda i, j: (layer, 0, j))],
        out_specs=pl.BlockSpec((bm, bn), lambda i, j: (i, j)),
        out_shape=jax.ShapeDtypeStruct((s, n), BF16),
        compiler_params=_params(
            ("parallel", "parallel"), [((bm, k), BF16), ((k, bn), w.dtype), ((bm, bn), BF16)],
            scratch=[((bm, bn), F32)]),
        name="mm_qk",
    )(h, w)


def _mm_nt_kernel(wt_ref, b_ref, h_ref, o_ref):
    acc = lax.dot_general(wt_ref[...], h_ref[...], (((1,), (1,)), ((), ())), preferred_element_type=F32)
    o_ref[0] = (acc + b_ref[...]).astype(o_ref.dtype)


def _mm_nt(h, wt, col_bias, bm, bn):
    s, k = h.shape
    n = wt.shape[0]
    return pl.pallas_call(
        _mm_nt_kernel,
        grid=(s // bm, n // bn),
        in_specs=[
            pl.BlockSpec((bn, k), lambda i, j: (j, 0)),
            pl.BlockSpec((bn, 1), lambda i, j: (j, 0)),
            pl.BlockSpec((bm, k), lambda i, j: (i, 0)),
        ],
        out_specs=pl.BlockSpec((1, bn, bm), lambda i, j: (i, j, 0)),
        out_shape=jax.ShapeDtypeStruct((s // bm, n, bm), BF16),
        compiler_params=_params(
            ("parallel", "parallel"),
            [((bn, k), BF16), ((bn, V7X_LANES), F32), ((bm, k), BF16), ((bn, bm), BF16)],
            scratch=[((bn, bm), F32)]),
        name="mm_nt",
    )(wt, col_bias, h)


def _value_weights(w_v, n_heads):
    k, d = w_v.shape
    hd = d // n_heads
    wt = jnp.pad(w_v.T.reshape(n_heads, hd, k), ((0, 0), (0, V_PAD), (0, 0)))
    ones_row = jnp.zeros((n_heads, hd + V_PAD, 1), F32).at[:, hd, :].set(1.0)
    return wt.reshape(n_heads * (hd + V_PAD), k), ones_row.reshape(n_heads * (hd + V_PAD), 1)


def _conv_ln_kernel(cur_ref, halo_ref, w_ref, b_ref, g_ref, beta_ref, o_ref, buf_ref, sh_ref, y_ref, *, taps):
    rows, d = cur_ref.shape
    n_tiles = d // V7X_LANES
    sub = V7X_F32_SUBLANES
    span = sh_ref.shape[2]
    first_block = pl.program_id(0) == 0
    for c in range(n_tiles):
        cs = slice(c * V7X_LANES, (c + 1) * V7X_LANES)
        halo = halo_ref[:, cs]
        buf_ref[c, 0:CONV_HALO, :] = jnp.where(first_block, jnp.zeros_like(halo), halo)
        buf_ref[c, CONV_HALO:CONV_HALO + rows, :] = cur_ref[:, cs]
        for s in range(1, sub):
            sh_ref[s - 1, c] = buf_ref[c, s:s + span, :]

    first = CONV_HALO - (taps - 1)

    def tile_body(c, carry):
        acc = jnp.broadcast_to(b_ref[c], (rows, V7X_LANES))
        for t in range(taps):
            base, s = divmod(first + t, sub)
            base *= sub
            src = buf_ref[c, base:base + rows, :] if s == 0 else sh_ref[s - 1, c, base:base + rows, :]
            acc = acc + w_ref[c, t:t + 1, :] * src
        y_ref[c] = acc
        return carry

    lax.fori_loop(0, n_tiles, tile_body, 0)

    total = y_ref[0]
    for c in range(1, n_tiles):
        total = total + y_ref[c]
    mu = jnp.sum(total, axis=-1, keepdims=True) * (1.0 / d)
    sq = jnp.zeros((rows, V7X_LANES), F32)
    for c in range(n_tiles):
        yc = y_ref[c] - mu
        sq = sq + yc * yc
    inv = lax.rsqrt(jnp.sum(sq, axis=-1, keepdims=True) * (1.0 / d) + EPS)
    for c in range(n_tiles):
        z = (y_ref[c] - mu) * inv * g_ref[c] + beta_ref[c]
        o_ref[:, c * V7X_LANES:(c + 1) * V7X_LANES] = (z * jax.nn.sigmoid(z)).astype(o_ref.dtype)


def _conv_ln(u, dw_w, dw_b, ln_g, ln_b):
    s, d = u.shape
    taps = dw_w.shape[0]
    rows = CONV_ROWS
    per = rows // CONV_HALO
    n_tiles = d // V7X_LANES
    vec = lambda v: v.reshape(n_tiles, 1, V7X_LANES)
    w_tiles = dw_w.reshape(taps, n_tiles, V7X_LANES).transpose(1, 0, 2)
    span = rows + ((CONV_HALO - 1) // V7X_F32_SUBLANES) * V7X_F32_SUBLANES
    buf = (n_tiles, rows + CONV_HALO, V7X_LANES)
    shifted = (V7X_F32_SUBLANES - 1, n_tiles, span, V7X_LANES)
    conv_out = (n_tiles, rows, V7X_LANES)
    vec_spec = pl.BlockSpec((n_tiles, 1, V7X_LANES), lambda i: (0, 0, 0))
    return pl.pallas_call(
        functools.partial(_conv_ln_kernel, taps=taps),
        grid=(s // rows,),
        in_specs=[
            pl.BlockSpec((rows, d), lambda i: (i, 0)),
            pl.BlockSpec((CONV_HALO, d), lambda i: (jnp.maximum(i * per - 1, 0), 0)),
            pl.BlockSpec((n_tiles, taps, V7X_LANES), lambda i: (0, 0, 0)),
            vec_spec, vec_spec, vec_spec,
        ],
        out_specs=pl.BlockSpec((rows, d), lambda i: (i, 0)),
        out_shape=jax.ShapeDtypeStruct((s, d), BF16),
        scratch_shapes=[pltpu.VMEM(buf, F32), pltpu.VMEM(shifted, F32), pltpu.VMEM(conv_out, F32)],
        compiler_params=_params(
            ("parallel",), [((rows, d), F32), ((CONV_HALO, d), F32), ((taps, d), F32), ((rows, d), BF16)],
            scratch=[(buf, F32), (shifted, F32), (conv_out, F32)]),
        name="conv_ln",
    )(u, u, w_tiles, vec(dw_b), vec(ln_g), vec(ln_b))


def _ffn_up_kernel(h_ref, hn_ref, wg_ref, wv_ref, cwg_ref, cwv_ref, cbg_ref, cbv_ref, o_ref,
                   wgb_ref, wvb_ref, buf_ref, gact_ref):
    rows = h_ref.shape[0]
    c = FFN_CARRY

    @pl.when(pl.program_id(1) == 0)
    def _():
        wgb_ref[...] = wg_ref[...].astype(BF16)
        wvb_ref[...] = wv_ref[...].astype(BF16)
        for part in range(2):
            buf_ref[part, 0:c, :] = jnp.zeros((c, buf_ref.shape[2]), F32)
        buf_ref[0, c:c + rows, :] = jnp.dot(h_ref[...], wgb_ref[...], preferred_element_type=F32)

    def conv(part, cw_ref, cb_ref):
        return (cw_ref[0:1, :] * buf_ref[part, c - 2:c - 2 + rows, :]
                + cw_ref[1:2, :] * buf_ref[part, c - 1:c - 1 + rows, :]
                + cw_ref[2:3, :] * buf_ref[part, c:c + rows, :]
                + cb_ref[...])

    gate = conv(0, cwg_ref, cbg_ref)
    gact_ref[...] = gate * jax.nn.sigmoid(gate)
    buf_ref[1, c:c + rows, :] = jnp.dot(h_ref[...], wvb_ref[...], preferred_element_type=F32)
    buf_ref[0, 0:c, :] = buf_ref[0, rows:rows + c, :]
    buf_ref[0, c:c + rows, :] = jnp.dot(hn_ref[...], wgb_ref[...], preferred_element_type=F32)
    val = conv(1, cwv_ref, cbv_ref)
    o_ref[...] = (gact_ref[...] * val).astype(o_ref.dtype)
    buf_ref[1, 0:c, :] = buf_ref[1, rows:rows + c, :]


def _ffn_up(h, w_up, layer, dw_w, dw_b, bm=1024, bn=512):
    s, k = h.shape
    f = w_up.shape[2] // 2
    nb = f // bn
    taps = dw_w.shape[0]
    assert taps - 1 <= FFN_CARRY
    dw_b = dw_b.reshape(1, 2 * f)
    buf = (2, bm + FFN_CARRY, bn)
    n_row = s // bm
    return pl.pallas_call(
        _ffn_up_kernel,
        grid=(nb, n_row),
        in_specs=[
            pl.BlockSpec((bm, k), lambda j, i: (i, 0)),
            pl.BlockSpec((bm, k), lambda j, i: (jnp.minimum(i + 1, n_row - 1), 0)),
            pl.BlockSpec((None, k, bn), lambda j, i: (layer, 0, j)),
            pl.BlockSpec((None, k, bn), lambda j, i: (layer, 0, j + nb)),
            pl.BlockSpec((taps, bn), lambda j, i: (0, j)),
            pl.BlockSpec((taps, bn), lambda j, i: (0, j + nb)),
            pl.BlockSpec((1, bn), lambda j, i: (0, j)),
            pl.BlockSpec((1, bn), lambda j, i: (0, j + nb)),
        ],
        out_specs=pl.BlockSpec((bm, bn), lambda j, i: (i, j)),
        out_shape=jax.ShapeDtypeStruct((s, f), BF16),
        scratch_shapes=[pltpu.VMEM((k, bn), BF16), pltpu.VMEM((k, bn), BF16), pltpu.VMEM(buf, F32),
                        pltpu.VMEM((bm, bn), F32)],
        compiler_params=_params(
            ("arbitrary", "arbitrary"),
            [((bm, k), BF16), ((bm, k), BF16), ((k, bn), w_up.dtype), ((k, bn), w_up.dtype), ((bm, bn), BF16)],
            scratch=[((k, bn), BF16), ((k, bn), BF16), (buf, F32), ((bm, bn), F32)]),
        name="ffn_up",
    )(h, h, w_up, w_up, dw_w, dw_w, dw_b, dw_b)


def _t5_bucket(rel):
    half = NUM_BUCKETS // 2
    max_exact = half // 2
    ret = jnp.where(rel > 0, half, 0)
    n = jnp.abs(rel)
    nf = jnp.maximum(n, 1).astype(jnp.float32)
    large = max_exact + (jnp.log(nf / max_exact) / math.log(MAX_DISTANCE / max_exact)
                         * (half - max_exact)).astype(jnp.int32)
    large = jnp.minimum(large, half - 1)
    return ret + jnp.where(n < max_exact, n, large)


def _far_bucket(block, seq):
    n = np.arange(block + 1, seq + 1, dtype=np.float64)
    half = NUM_BUCKETS // 2
    max_exact = half // 2
    large = max_exact + (np.log(n / max_exact) / math.log(MAX_DISTANCE / max_exact) * (half - max_exact)).astype(np.int64)
    assert large.min() > half - 1, "attention block too small for a constant far-key bias"
    return half - 1


def _bias_tiles_kernel(tbl_ref, bucket_ref, allowed_ref, o_ref, *, far_bucket):
    h = pl.program_id(0)
    far = tbl_ref[h, far_bucket]
    for t in range(2):
        bucket = bucket_ref[t]
        acc = jnp.zeros(bucket.shape, F32)
        for b in range(NUM_BUCKETS if t == 0 else NUM_BUCKETS // 2):
            acc = jnp.where(bucket == b, (tbl_ref[h, b] - far) * LOG2E, acc)
        o_ref[0, t] = jnp.where(allowed_ref[t] != 0, acc, -jnp.inf)


def _bias_tiles(rel_bias, block, seq):
    n_heads = rel_bias.shape[1]
    key = jnp.arange(block, dtype=jnp.int32)[:, None]
    qry = jnp.arange(block, dtype=jnp.int32)[None, :]
    rel = jnp.stack([key - qry, key - qry - block])
    bucket = _t5_bucket(rel)
    allowed = jnp.stack([(key // CHUNK) <= (qry // CHUNK), jnp.ones((block, block), bool)]).astype(jnp.int32)
    return pl.pallas_call(
        functools.partial(_bias_tiles_kernel, far_bucket=_far_bucket(block, seq)),
        grid=(n_heads,),
        in_specs=[
            pl.BlockSpec(memory_space=pltpu.SMEM),
            pl.BlockSpec((2, block, block), lambda h: (0, 0, 0)),
            pl.BlockSpec((2, block, block), lambda h: (0, 0, 0)),
        ],
        out_specs=pl.BlockSpec((1, 2, block, block), lambda h: (h, 0, 0, 0)),
        out_shape=jax.ShapeDtypeStruct((n_heads, 2, block, block), F32),
        compiler_params=_params(
            ("parallel",),
            [((2, block, block), jnp.int32), ((2, block, block), jnp.int32), ((2, block, block), F32)]),
        name="bias_tiles",
    )(rel_bias.T, bucket, allowed)


def _attn_kernel(q_ref, qn_ref, k_ref, vt_ref, bias_ref, lam_ref, g_ref, o_ref, qp_ref, s_ref, m_ref, acc_ref, *,
                 lambda_init):
    blk, hd = q_ref.shape
    qi = pl.program_id(1)
    q = q_ref[...]
    lane = lax.broadcasted_iota(jnp.int32, q.shape, 1)
    zero = jnp.zeros_like(q)
    qp_ref[0] = jnp.where(lane < HEAD_DIM, q, zero)
    qp_ref[1] = jnp.where(lane >= HEAD_DIM, q, zero)
    m_ref[...] = jnp.full(m_ref.shape, -jnp.inf, F32)
    acc_ref[...] = jnp.zeros(acc_ref.shape, F32)

    def scores(c, kb):
        k = k_ref[pl.ds(pl.multiple_of(kb * blk, blk), blk), :]
        s_ref[c] = lax.dot_general(k, qp_ref[c], (((1,), (1,)), ((), ())), preferred_element_type=F32)

    def accumulate(c, kb, bias_tile):
        vt = vt_ref[kb]
        for j in range(blk // ATTN_QCHUNK):
            cs = pl.ds(j * ATTN_QCHUNK, ATTN_QCHUNK)
            s = s_ref[c, :, cs]
            if bias_tile is not None:
                s = s + bias_ref[0, bias_tile, :, cs]
            m_old = m_ref[c, :, cs]
            m_new = jnp.maximum(m_old, jnp.max(s, axis=0, keepdims=True))
            alpha = jnp.exp2(m_old - m_new)
            p = jnp.exp2(s - m_new).astype(vt.dtype)
            acc_ref[c, :, cs] = alpha * acc_ref[c, :, cs] + jnp.dot(vt, p, preferred_element_type=F32)
            m_ref[c, :, cs] = m_new

    def block_step(kb, bias_tile, has_next):
        scores(1, kb)
        accumulate(0, kb, bias_tile)
        if has_next:
            scores(0, kb + 1)
        accumulate(1, kb, bias_tile)

    @pl.when(qi == 0)
    def _():
        scores(0, 0)

    def far_blocks(first, count):
        for u in range(count):
            block_step(first + u, None, True)

    def far_group(group, carry):
        far_blocks(ATTN_UNROLL * group, ATTN_UNROLL)
        return carry

    n_far = jnp.maximum(qi - 1, 0)
    n_grouped = (n_far // ATTN_UNROLL) * ATTN_UNROLL
    lax.fori_loop(0, n_far // ATTN_UNROLL, far_group, 0)
    rest = n_far - n_grouped
    size = ATTN_UNROLL // 2
    while size >= 1:
        first = n_grouped + (rest // (2 * size)) * (2 * size)
        pl.when((rest // size) % 2 == 1)(functools.partial(far_blocks, first, size))
        size //= 2

    @pl.when(qi >= 1)
    def _():
        block_step(qi - 1, 1, True)
        block_step(qi, 0, False)

    @pl.when(qi == 0)
    def _():
        block_step(qi, 0, False)

    q_next = qn_ref[...]
    s_ref[0] = lax.dot_general(k_ref[0:blk, :], jnp.where(lane < HEAD_DIM, q_next, jnp.zeros_like(q_next)),
                               (((1,), (1,)), ((), ())), preferred_element_type=F32)

    lam_v = lam_ref[...]
    lam = (jnp.exp(jnp.sum(lam_v[0:1] * lam_v[1:2], axis=-1, keepdims=True))
           - jnp.exp(jnp.sum(lam_v[2:3] * lam_v[3:4], axis=-1, keepdims=True)) + lambda_init)
    o = (acc_ref[0, 0:hd, :] / acc_ref[0, hd:hd + 1, :]
         - lam * (acc_ref[1, 0:hd, :] / acc_ref[1, hd:hd + 1, :]))
    ms = jnp.mean(o * o, axis=0, keepdims=True)
    y = o * lax.rsqrt(ms + EPS) * g_ref[...] * (1.0 - lambda_init)
    o_ref[...] = y.T.astype(o_ref.dtype)


def _attention(qk, vt, bias, lam_vecs, subln_g, lambda_init):
    s, d2 = qk.shape
    d = d2 // 2
    nb, _, blk = vt.shape
    hd = 2 * HEAD_DIM
    hv = hd + V_PAD
    n_heads = d // hd
    return pl.pallas_call(
        functools.partial(_attn_kernel, lambda_init=lambda_init),
        grid=(n_heads, nb),
        in_specs=[
            pl.BlockSpec((blk, hd), lambda h, i: (i, h)),
            pl.BlockSpec((blk, hd), lambda h, i: (jnp.minimum(i + 1, nb - 1), h)),
            pl.BlockSpec((s, hd), lambda h, i: (0, n_heads + h)),
            pl.BlockSpec((nb, hv, blk), lambda h, i: (0, h, 0)),
            pl.BlockSpec((1, 2, blk, blk), lambda h, i: (h, 0, 0, 0)),
            pl.BlockSpec((4, HEAD_DIM), lambda h, i: (0, 0)),
            pl.BlockSpec((hd, 1), lambda h, i: (0, 0)),
        ],
        out_specs=pl.BlockSpec((blk, hd), lambda h, i: (i, h)),
        out_shape=jax.ShapeDtypeStruct((s, d), BF16),
        scratch_shapes=[
            pltpu.VMEM((2, blk, hd), BF16), pltpu.VMEM((2, blk, blk), F32),
            pltpu.VMEM((2, 1, blk), F32), pltpu.VMEM((2, hv, blk), F32)],
        compiler_params=_params(
            ("arbitrary", "arbitrary"),
            [((blk, hd), BF16), ((blk, hd), BF16), ((s, hd), BF16), ((nb, hv, blk), BF16), ((2, blk, blk), F32),
             ((blk, hd), BF16)],
            scratch=[((2, blk, hd), BF16), ((2, blk, blk), F32), ((2, hv, blk), F32)]),
        name="diff_attention",
    )(qk, qk, qk, vt, bias, lam_vecs, subln_g.reshape(hd, 1))


def kernel(x, mix_norm, ffn_norm, conv_w_in, conv_b_in, conv_dw_w, conv_dw_b, conv_ln_g, conv_ln_b, conv_w_out, conv_b_out, attn_w_qkv, attn_lambda_q1, attn_lambda_k1, attn_lambda_q2, attn_lambda_k2, attn_subln_g, attn_w_o, rel_bias, ffn_w_up, ffn_dw_w, ffn_dw_b, ffn_w_down, final_norm_g):
    batch, seq, d = x.shape
    assert batch == 1
    depth = mix_norm.shape[0]
    x = x.reshape(seq, d)
    ffn_w_down = ffn_w_down.astype(BF16)
    conv_w_out = conv_w_out.astype(BF16)
    attn_w_o = attn_w_o.astype(BF16)
    h = None
    for i in range(depth):
        j = i // N_MIXERS
        if i % N_MIXERS == 0:
            u = _mm_glu(x, mix_norm[i], conv_w_in, j, conv_b_in[j])
            c = _conv_ln(u, conv_dw_w[j], conv_dw_b[j], conv_ln_g[j], conv_ln_b[j])
            x, h = _mm_res_norm(c, conv_w_out, j, x, ffn_norm[i], bias=conv_b_out[j])
        else:
            lambda_init = 0.8 - 0.6 * math.exp(-0.3 * i)
            n_heads = d // (2 * HEAD_DIM)
            qk = _mm_qk(h, attn_w_qkv, j, n=2 * d, n_scaled=d, scale=HEAD_DIM ** -0.5 * LOG2E)
            wt_v, ones_col = _value_weights(attn_w_qkv[j, :, 2 * d:].astype(BF16), n_heads)
            vt = _mm_nt(h, wt_v, ones_col, bm=ATTN_BLOCK, bn=wt_v.shape[0] // 2)
            bias = _bias_tiles(rel_bias, ATTN_BLOCK, seq)
            lam_vecs = jnp.stack([attn_lambda_q1[j], attn_lambda_k1[j], attn_lambda_q2[j], attn_lambda_k2[j]])
            o = _attention(qk, vt, bias, lam_vecs, attn_subln_g[j], lambda_init)
            x, h = _mm_res_norm(o, attn_w_o, j, x, ffn_norm[i])
        act = _ffn_up(h, ffn_w_up, i, ffn_dw_w[i], ffn_dw_b[i])
        last = i == depth - 1
        x, h = _mm_res_norm(act, ffn_w_down, i, x, final_norm_g if last else mix_norm[i + 1],
                            out_dtype=F32 if last else BF16, emit_x=not last)
    return h.reshape(batch, seq, d)
```

```python
import functools
import math

import numpy as np
import jax
import jax.numpy as jnp
from jax import lax
from jax.experimental import pallas as pl
from jax.experimental.pallas import tpu as pltpu

F32 = jnp.float32
BF16 = jnp.bfloat16

CHUNK = 64
HEAD_DIM = 64
NUM_BUCKETS = 32
MAX_DISTANCE = 128
EPS = 1e-6
N_MIXERS = 2

V7X_VMEM_BYTES = 64 * 1024 * 1024
V7X_F32_SUBLANES = 8
V7X_LANES = 128
V7X_MXU_WIDTH = 256
VMEM_TEMP_BYTES = 12 * 1024 * 1024

LOG2E = math.log2(math.e)

ATTN_BLOCK = 512
ATTN_UNROLL = 16
ATTN_QCHUNK = 256
V_PAD = 16
CONV_ROWS = 256
CONV_HALO = 32
FFN_CARRY = V7X_F32_SUBLANES


def _nbytes(shape, dtype):
    return int(np.prod(shape)) * jnp.dtype(dtype).itemsize


def _params(semantics, blocks, scratch=()):
    need = 2 * sum(_nbytes(s, d) for s, d in blocks) + sum(_nbytes(s, d) for s, d in scratch)
    limit = min(need + VMEM_TEMP_BYTES, V7X_VMEM_BYTES - 4 * 1024 * 1024)
    return pltpu.CompilerParams(dimension_semantics=semantics, vmem_limit_bytes=limit)


def _mm_glu_kernel(x_ref, ng_ref, wa_ref, wg_ref, ba_ref, bg_ref, o_ref, h_ref):
    @pl.when(pl.program_id(1) == 0)
    def _():
        x = x_ref[...]
        ms = jnp.mean(x * x, axis=-1, keepdims=True)
        h_ref[...] = (x * lax.rsqrt(ms + EPS) * ng_ref[...]).astype(h_ref.dtype)

    h = h_ref[...]
    a = jnp.dot(h, wa_ref[...].astype(BF16), preferred_element_type=F32) + ba_ref[...]
    g = jnp.dot(h, wg_ref[...].astype(BF16), preferred_element_type=F32) + bg_ref[...]
    o_ref[...] = a * jax.nn.sigmoid(g)


def _mm_glu(x, norm_g, w, layer, b, bm=1024, bn=512):
    s, k = x.shape
    n = w.shape[2] // 2
    nb = n // bn
    b = b.reshape(1, 2 * n)
    return pl.pallas_call(
        _mm_glu_kernel,
        grid=(s // bm, nb),
        in_specs=[
            pl.BlockSpec((bm, k), lambda i, j: (i, 0)),
            pl.BlockSpec((1, k), lambda i, j: (0, 0)),
            pl.BlockSpec((None, k, bn), lambda i, j: (layer, 0, j)),
            pl.BlockSpec((None, k, bn), lambda i, j: (layer, 0, j + nb)),
            pl.BlockSpec((1, bn), lambda i, j: (0, j)),
            pl.BlockSpec((1, bn), lambda i, j: (0, j + nb)),
        ],
        out_specs=pl.BlockSpec((bm, bn), lambda i, j: (i, j)),
        out_shape=jax.ShapeDtypeStruct((s, n), F32),
        scratch_shapes=[pltpu.VMEM((bm, k), BF16)],
        compiler_params=_params(
            ("parallel", "arbitrary"),
            [((bm, k), F32), ((k, bn), w.dtype), ((k, bn), w.dtype), ((bm, bn), F32)],
            scratch=[((bm, k), BF16)]),
        name="mm_glu",
    )(x, norm_g.reshape(1, k), w, w, b, b)


def _mm_res_norm_kernel(a_ref, w_ref, *refs, has_bias, emit_x):
    refs = list(refs)
    b_ref = refs.pop(0) if has_bias else None
    r_ref, g_ref = refs.pop(0), refs.pop(0)
    xo_ref = refs.pop(0) if emit_x else None
    ho_ref = refs.pop(0)
    y = jnp.dot(a_ref[...], w_ref[...], preferred_element_type=F32)
    if has_bias:
        y = y + b_ref[...]
    y = r_ref[...] + y
    if emit_x:
        xo_ref[...] = y
    ms = jnp.mean(y * y, axis=-1, keepdims=True)
    ho_ref[...] = (y * lax.rsqrt(ms + EPS) * g_ref[...]).astype(ho_ref.dtype)


def _mm_res_norm(a, w, layer, res, norm_g, bias=None, out_dtype=BF16, emit_x=True):
    s, k = a.shape
    n = w.shape[2]
    budget = V7X_VMEM_BYTES - 8 * 1024 * 1024 - VMEM_TEMP_BYTES
    w_bytes = _nbytes((k, n), BF16)
    w_buffers = 2 if 2 * w_bytes <= budget // 2 else 1
    row_bytes = lambda bm: (_nbytes((bm, k), BF16) + _nbytes((bm, n), F32) * (2 if emit_x else 1)
                            + _nbytes((bm, n), out_dtype))
    bm = next(c for c in (512, 256, 128) if w_buffers * w_bytes + 2 * row_bytes(c) <= budget)
    row_spec = lambda cols: pl.BlockSpec((bm, cols), lambda i: (i, 0))
    vec_spec = pl.BlockSpec((1, n), lambda i: (0, 0))
    w_mode = {} if w_buffers == 2 else {"pipeline_mode": pl.Buffered(1)}
    specs = [row_spec(k), pl.BlockSpec((None, k, n), lambda i: (layer, 0, 0), **w_mode)]
    args = [a, w]
    if bias is not None:
        specs.append(vec_spec)
        args.append(bias.reshape(1, n))
    specs += [row_spec(n), vec_spec]
    args += [res, norm_g.reshape(1, n)]
    outs = [jax.ShapeDtypeStruct((s, n), out_dtype)]
    if emit_x:
        outs.insert(0, jax.ShapeDtypeStruct((s, n), F32))
    limit = w_buffers * w_bytes + 2 * row_bytes(bm) + VMEM_TEMP_BYTES
    out = pl.pallas_call(
        functools.partial(_mm_res_norm_kernel, has_bias=bias is not None, emit_x=emit_x),
        grid=(s // bm,),
        in_specs=specs,
        out_specs=[row_spec(n)] * len(outs),
        out_shape=outs,
        compiler_params=pltpu.CompilerParams(dimension_semantics=("parallel",), vmem_limit_bytes=limit),
        name="mm_res_norm",
    )(*args)
    return (out[0], out[1]) if emit_x else (None, out[0])


def _mm_qk_kernel(h_ref, w_ref, o_ref, *, n_scaled_blocks, scale):
    acc = jnp.dot(h_ref[...], w_ref[...].astype(BF16), preferred_element_type=F32)
    mult = jnp.where(pl.program_id(1) < n_scaled_blocks, scale, 1.0).astype(F32)
    o_ref[...] = (acc * mult).astype(o_ref.dtype)


def _mm_qk(h, w, layer, n, n_scaled, scale, bm=1024, bn=1024):
    s, k = h.shape
    return pl.pallas_call(
        functools.partial(_mm_qk_kernel, n_scaled_blocks=n_scaled // bn, scale=scale),
        grid=(s // bm, n // bn),
        in_specs=[pl.BlockSpec((bm, k), lambda i, j: (i, 0)),
                  pl.BlockSpec((None, k, bn), lambda i, j: (layer, 0, j))],
        out_specs=pl.BlockSpec((bm, bn), lambda i, j: (i, j)),
        out_shape=jax.ShapeDtypeStruct((s, n), BF16),
        compiler_params=_params(
            ("parallel", "parallel"), [((bm, k), BF16), ((k, bn), w.dtype), ((bm, bn), BF16)],
            scratch=[((bm, bn), F32)]),
        name="mm_qk",
    )(h, w)


def _mm_nt_kernel(wt_ref, b_ref, h_ref, o_ref):
    acc = lax.dot_general(wt_ref[...], h_ref[...], (((1,), (1,)), ((), ())), preferred_element_type=F32)
    o_ref[0] = (acc + b_ref[...]).astype(o_ref.dtype)


def _mm_nt(h, wt, col_bias, bm, bn):
    s, k = h.shape
    n = wt.shape[0]
    return pl.pallas_call(
        _mm_nt_kernel,
        grid=(s // bm, n // bn),
        in_specs=[
            pl.BlockSpec((bn, k), lambda i, j: (j, 0)),
            pl.BlockSpec((bn, 1), lambda i, j: (j, 0)),
            pl.BlockSpec((bm, k), lambda i, j: (i, 0)),
        ],
        out_specs=pl.BlockSpec((1, bn, bm), lambda i, j: (i, j, 0)),
        out_shape=jax.ShapeDtypeStruct((s // bm, n, bm), BF16),
        compiler_params=_params(
            ("parallel", "parallel"),
            [((bn, k), BF16), ((bn, V7X_LANES), F32), ((bm, k), BF16), ((bn, bm), BF16)],
            scratch=[((bn, bm), F32)]),
        name="mm_nt",
    )(wt, col_bias, h)


def _value_weights(w_v, n_heads):
    k, d = w_v.shape
    hd = d // n_heads
    wt = jnp.pad(w_v.T.reshape(n_heads, hd, k), ((0, 0), (0, V_PAD), (0, 0)))
    ones_row = jnp.zeros((n_heads, hd + V_PAD, 1), F32).at[:, hd, :].set(1.0)
    return wt.reshape(n_heads * (hd + V_PAD), k), ones_row.reshape(n_heads * (hd + V_PAD), 1)


def _conv_ln_kernel(cur_ref, halo_ref, w_ref, b_ref, g_ref, beta_ref, o_ref, buf_ref, sh_ref, y_ref, *, taps):
    rows, d = cur_ref.shape
    n_tiles = d // V7X_LANES
    sub = V7X_F32_SUBLANES
    span = sh_ref.shape[2]
    first_block = pl.program_id(0) == 0
    for c in range(n_tiles):
        cs = slice(c * V7X_LANES, (c + 1) * V7X_LANES)
        halo = halo_ref[:, cs]
        buf_ref[c, 0:CONV_HALO, :] = jnp.where(first_block, jnp.zeros_like(halo), halo)
        buf_ref[c, CONV_HALO:CONV_HALO + rows, :] = cur_ref[:, cs]
        for s in range(1, sub):
            sh_ref[s - 1, c] = buf_ref[c, s:s + span, :]

    first = CONV_HALO - (taps - 1)

    def tile_body(c, carry):
        acc = jnp.broadcast_to(b_ref[c], (rows, V7X_LANES))
        for t in range(taps):
            base, s = divmod(first + t, sub)
            base *= sub
            src = buf_ref[c, base:base + rows, :] if s == 0 else sh_ref[s - 1, c, base:base + rows, :]
            acc = acc + w_ref[c, t:t + 1, :] * src
        y_ref[c] = acc
        return carry

    lax.fori_loop(0, n_tiles, tile_body, 0)

    total = y_ref[0]
    for c in range(1, n_tiles):
        total = total + y_ref[c]
    mu = jnp.sum(total, axis=-1, keepdims=True) * (1.0 / d)
    sq = jnp.zeros((rows, V7X_LANES), F32)
    for c in range(n_tiles):
        yc = y_ref[c] - mu
        sq = sq + yc * yc
    inv = lax.rsqrt(jnp.sum(sq, axis=-1, keepdims=True) * (1.0 / d) + EPS)
    for c in range(n_tiles):
        z = (y_ref[c] - mu) * inv * g_ref[c] + beta_ref[c]
        o_ref[:, c * V7X_LANES:(c + 1) * V7X_LANES] = (z * jax.nn.sigmoid(z)).astype(o_ref.dtype)


def _conv_ln(u, dw_w, dw_b, ln_g, ln_b):
    s, d = u.shape
    taps = dw_w.shape[0]
    rows = CONV_ROWS
    per = rows // CONV_HALO
    n_tiles = d // V7X_LANES
    vec = lambda v: v.reshape(n_tiles, 1, V7X_LANES)
    w_tiles = dw_w.reshape(taps, n_tiles, V7X_LANES).transpose(1, 0, 2)
    span = rows + ((CONV_HALO - 1) // V7X_F32_SUBLANES) * V7X_F32_SUBLANES
    buf = (n_tiles, rows + CONV_HALO, V7X_LANES)
    shifted = (V7X_F32_SUBLANES - 1, n_tiles, span, V7X_LANES)
    conv_out = (n_tiles, rows, V7X_LANES)
    vec_spec = pl.BlockSpec((n_tiles, 1, V7X_LANES), lambda i: (0, 0, 0))
    return pl.pallas_call(
        functools.partial(_conv_ln_kernel, taps=taps),
        grid=(s // rows,),
        in_specs=[
            pl.BlockSpec((rows, d), lambda i: (i, 0)),
            pl.BlockSpec((CONV_HALO, d), lambda i: (jnp.maximum(i * per - 1, 0), 0)),
            pl.BlockSpec((n_tiles, taps, V7X_LANES), lambda i: (0, 0, 0)),
            vec_spec, vec_spec, vec_spec,
        ],
        out_specs=pl.BlockSpec((rows, d), lambda i: (i, 0)),
        out_shape=jax.ShapeDtypeStruct((s, d), BF16),
        scratch_shapes=[pltpu.VMEM(buf, F32), pltpu.VMEM(shifted, F32), pltpu.VMEM(conv_out, F32)],
        compiler_params=_params(
            ("parallel",), [((rows, d), F32), ((CONV_HALO, d), F32), ((taps, d), F32), ((rows, d), BF16)],
            scratch=[(buf, F32), (shifted, F32), (conv_out, F32)]),
        name="conv_ln",
    )(u, u, w_tiles, vec(dw_b), vec(ln_g), vec(ln_b))


def _ffn_up_kernel(h_ref, wg_ref, wv_ref, cwg_ref, cwv_ref, cbg_ref, cbv_ref, o_ref,
                   wgb_ref, wvb_ref, buf_ref):
    rows = h_ref.shape[0]
    c = FFN_CARRY

    @pl.when(pl.program_id(1) == 0)
    def _():
        wgb_ref[...] = wg_ref[...].astype(BF16)
        wvb_ref[...] = wv_ref[...].astype(BF16)
        for part in range(2):
            buf_ref[part, 0:c, :] = jnp.zeros((c, buf_ref.shape[2]), F32)

    h = h_ref[...]
    buf_ref[0, c:c + rows, :] = jnp.dot(h, wgb_ref[...], preferred_element_type=F32)
    buf_ref[1, c:c + rows, :] = jnp.dot(h, wvb_ref[...], preferred_element_type=F32)

    def conv(part, cw_ref, cb_ref):
        return (cw_ref[0:1, :] * buf_ref[part, c - 2:c - 2 + rows, :]
                + cw_ref[1:2, :] * buf_ref[part, c - 1:c - 1 + rows, :]
                + cw_ref[2:3, :] * buf_ref[part, c:c + rows, :]
                + cb_ref[...])

    gate = conv(0, cwg_ref, cbg_ref)
    val = conv(1, cwv_ref, cbv_ref)
    o_ref[...] = (gate * jax.nn.sigmoid(gate) * val).astype(o_ref.dtype)
    for part in range(2):
        buf_ref[part, 0:c, :] = buf_ref[part, rows:rows + c, :]


def _ffn_up(h, w_up, layer, dw_w, dw_b, bm=1024, bn=512):
    s, k = h.shape
    f = w_up.shape[2] // 2
    nb = f // bn
    taps = dw_w.shape[0]
    assert taps - 1 <= FFN_CARRY
    dw_b = dw_b.reshape(1, 2 * f)
    buf = (2, bm + FFN_CARRY, bn)
    return pl.pallas_call(
        _ffn_up_kernel,
        grid=(nb, s // bm),
        in_specs=[
            pl.BlockSpec((bm, k), lambda j, i: (i, 0)),
            pl.BlockSpec((None, k, bn), lambda j, i: (layer, 0, j)),
            pl.BlockSpec((None, k, bn), lambda j, i: (layer, 0, j + nb)),
            pl.BlockSpec((taps, bn), lambda j, i: (0, j)),
            pl.BlockSpec((taps, bn), lambda j, i: (0, j + nb)),
            pl.BlockSpec((1, bn), lambda j, i: (0, j)),
            pl.BlockSpec((1, bn), lambda j, i: (0, j + nb)),
        ],
        out_specs=pl.BlockSpec((bm, bn), lambda j, i: (i, j)),
        out_shape=jax.ShapeDtypeStruct((s, f), BF16),
        scratch_shapes=[pltpu.VMEM((k, bn), BF16), pltpu.VMEM((k, bn), BF16), pltpu.VMEM(buf, F32)],
        compiler_params=_params(
            ("arbitrary", "arbitrary"),
            [((bm, k), BF16), ((k, bn), w_up.dtype), ((k, bn), w_up.dtype), ((bm, bn), BF16)],
            scratch=[((k, bn), BF16), ((k, bn), BF16), (buf, F32)]),
        name="ffn_up",
    )(h, w_up, w_up, dw_w, dw_w, dw_b, dw_b)


def _t5_bucket(rel):
    half = NUM_BUCKETS // 2
    max_exact = half // 2
    ret = jnp.where(rel > 0, half, 0)
    n = jnp.abs(rel)
    nf = jnp.maximum(n, 1).astype(jnp.float32)
    large = max_exact + (jnp.log(nf / max_exact) / math.log(MAX_DISTANCE / max_exact)
                         * (half - max_exact)).astype(jnp.int32)
    large = jnp.minimum(large, half - 1)
    return ret + jnp.where(n < max_exact, n, large)


def _far_bucket(block, seq):
    n = np.arange(block + 1, seq + 1, dtype=np.float64)
    half = NUM_BUCKETS // 2
    max_exact = half // 2
    large = max_exact + (np.log(n / max_exact) / math.log(MAX_DISTANCE / max_exact) * (half - max_exact)).astype(np.int64)
    assert large.min() > half - 1, "attention block too small for a constant far-key bias"
    return half - 1


def _bias_tiles_kernel(tbl_ref, bucket_ref, allowed_ref, o_ref, *, far_bucket):
    h = pl.program_id(0)
    far = tbl_ref[h, far_bucket]
    for t in range(2):
        bucket = bucket_ref[t]
        acc = jnp.zeros(bucket.shape, F32)
        for b in range(NUM_BUCKETS if t == 0 else NUM_BUCKETS // 2):
            acc = jnp.where(bucket == b, (tbl_ref[h, b] - far) * LOG2E, acc)
        o_ref[0, t] = jnp.where(allowed_ref[t] != 0, acc, -jnp.inf)


def _bias_tiles(rel_bias, block, seq):
    n_heads = rel_bias.shape[1]
    key = jnp.arange(block, dtype=jnp.int32)[:, None]
    qry = jnp.arange(block, dtype=jnp.int32)[None, :]
    rel = jnp.stack([key - qry, key - qry - block])
    bucket = _t5_bucket(rel)
    allowed = jnp.stack([(key // CHUNK) <= (qry // CHUNK), jnp.ones((block, block), bool)]).astype(jnp.int32)
    return pl.pallas_call(
        functools.partial(_bias_tiles_kernel, far_bucket=_far_bucket(block, seq)),
        grid=(n_heads,),
        in_specs=[
            pl.BlockSpec(memory_space=pltpu.SMEM),
            pl.BlockSpec((2, block, block), lambda h: (0, 0, 0)),
            pl.BlockSpec((2, block, block), lambda h: (0, 0, 0)),
        ],
        out_specs=pl.BlockSpec((1, 2, block, block), lambda h: (h, 0, 0, 0)),
        out_shape=jax.ShapeDtypeStruct((n_heads, 2, block, block), F32),
        compiler_params=_params(
            ("parallel",),
            [((2, block, block), jnp.int32), ((2, block, block), jnp.int32), ((2, block, block), F32)]),
        name="bias_tiles",
    )(rel_bias.T, bucket, allowed)


def _attn_kernel(q_ref, qn_ref, k_ref, vt_ref, bias_ref, lam_ref, g_ref, o_ref, qp_ref, s_ref, m_ref, acc_ref, *,
                 lambda_init):
    blk, hd = q_ref.shape
    qi = pl.program_id(1)
    q = q_ref[...]
    lane = lax.broadcasted_iota(jnp.int32, q.shape, 1)
    zero = jnp.zeros_like(q)
    qp_ref[0] = jnp.where(lane < HEAD_DIM, q, zero)
    qp_ref[1] = jnp.where(lane >= HEAD_DIM, q, zero)
    m_ref[...] = jnp.full(m_ref.shape, -jnp.inf, F32)
    acc_ref[...] = jnp.zeros(acc_ref.shape, F32)

    def scores(c, kb):
        k = k_ref[pl.ds(pl.multiple_of(kb * blk, blk), blk), :]
        s_ref[c] = lax.dot_general(k, qp_ref[c], (((1,), (1,)), ((), ())), preferred_element_type=F32)

    def accumulate(c, kb, bias_tile):
        vt = vt_ref[kb]
        for j in range(blk // ATTN_QCHUNK):
            cs = pl.ds(j * ATTN_QCHUNK, ATTN_QCHUNK)
            s = s_ref[c, :, cs]
            if bias_tile is not None:
                s = s + bias_ref[0, bias_tile, :, cs]
            m_old = m_ref[c, :, cs]
            m_new = jnp.maximum(m_old, jnp.max(s, axis=0, keepdims=True))
            alpha = jnp.exp2(m_old - m_new)
            p = jnp.exp2(s - m_new).astype(vt.dtype)
            acc_ref[c, :, cs] = alpha * acc_ref[c, :, cs] + jnp.dot(vt, p, preferred_element_type=F32)
            m_ref[c, :, cs] = m_new

    def block_step(kb, bias_tile, has_next):
        scores(1, kb)
        accumulate(0, kb, bias_tile)
        if has_next:
            scores(0, kb + 1)
        accumulate(1, kb, bias_tile)

    @pl.when(qi == 0)
    def _():
        scores(0, 0)

    def far_blocks(first, count):
        for u in range(count):
            block_step(first + u, None, True)

    def far_group(group, carry):
        far_blocks(ATTN_UNROLL * group, ATTN_UNROLL)
        return carry

    n_far = jnp.maximum(qi - 1, 0)
    n_grouped = (n_far // ATTN_UNROLL) * ATTN_UNROLL
    lax.fori_loop(0, n_far // ATTN_UNROLL, far_group, 0)
    rest = n_far - n_grouped
    size = ATTN_UNROLL // 2
    while size >= 1:
        first = n_grouped + (rest // (2 * size)) * (2 * size)
        pl.when((rest // size) % 2 == 1)(functools.partial(far_blocks, first, size))
        size //= 2

    @pl.when(qi >= 1)
    def _():
        block_step(qi - 1, 1, True)
        block_step(qi, 0, False)

    @pl.when(qi == 0)
    def _():
        block_step(qi, 0, False)

    q_next = qn_ref[...]
    s_ref[0] = lax.dot_general(k_ref[0:blk, :], jnp.where(lane < HEAD_DIM, q_next, jnp.zeros_like(q_next)),
                               (((1,), (1,)), ((), ())), preferred_element_type=F32)

    lam_v = lam_ref[...]
    lam = (jnp.exp(jnp.sum(lam_v[0:1] * lam_v[1:2], axis=-1, keepdims=True))
           - jnp.exp(jnp.sum(lam_v[2:3] * lam_v[3:4], axis=-1, keepdims=True)) + lambda_init)
    o = (acc_ref[0, 0:hd, :] / acc_ref[0, hd:hd + 1, :]
         - lam * (acc_ref[1, 0:hd, :] / acc_ref[1, hd:hd + 1, :]))
    ms = jnp.mean(o * o, axis=0, keepdims=True)
    y = o * lax.rsqrt(ms + EPS) * g_ref[...] * (1.0 - lambda_init)
    o_ref[...] = y.T.astype(o_ref.dtype)


def _attention(qk, vt, bias, lam_vecs, subln_g, lambda_init):
    s, d2 = qk.shape
    d = d2 // 2
    nb, _, blk = vt.shape
    hd = 2 * HEAD_DIM
    hv = hd + V_PAD
    n_heads = d // hd
    return pl.pallas_call(
        functools.partial(_attn_kernel, lambda_init=lambda_init),
        grid=(n_heads, nb),
        in_specs=[
            pl.BlockSpec((blk, hd), lambda h, i: (i, h)),
            pl.BlockSpec((blk, hd), lambda h, i: (jnp.minimum(i + 1, nb - 1), h)),
            pl.BlockSpec((s, hd), lambda h, i: (0, n_heads + h)),
            pl.BlockSpec((nb, hv, blk), lambda h, i: (0, h, 0)),
            pl.BlockSpec((1, 2, blk, blk), lambda h, i: (h, 0, 0, 0)),
            pl.BlockSpec((4, HEAD_DIM), lambda h, i: (0, 0)),
            pl.BlockSpec((hd, 1), lambda h, i: (0, 0)),
        ],
        out_specs=pl.BlockSpec((blk, hd), lambda h, i: (i, h)),
        out_shape=jax.ShapeDtypeStruct((s, d), BF16),
        scratch_shapes=[
            pltpu.VMEM((2, blk, hd), BF16), pltpu.VMEM((2, blk, blk), F32),
            pltpu.VMEM((2, 1, blk), F32), pltpu.VMEM((2, hv, blk), F32)],
        compiler_params=_params(
            ("arbitrary", "arbitrary"),
            [((blk, hd), BF16), ((blk, hd), BF16), ((s, hd), BF16), ((nb, hv, blk), BF16), ((2, blk, blk), F32),
             ((blk, hd), BF16)],
            scratch=[((2, blk, hd), BF16), ((2, blk, blk), F32), ((2, hv, blk), F32)]),
        name="diff_attention",
    )(qk, qk, qk, vt, bias, lam_vecs, subln_g.reshape(hd, 1))


def kernel(x, mix_norm, ffn_norm, conv_w_in, conv_b_in, conv_dw_w, conv_dw_b, conv_ln_g, conv_ln_b, conv_w_out, conv_b_out, attn_w_qkv, attn_lambda_q1, attn_lambda_k1, attn_lambda_q2, attn_lambda_k2, attn_subln_g, attn_w_o, rel_bias, ffn_w_up, ffn_dw_w, ffn_dw_b, ffn_w_down, final_norm_g):
    batch, seq, d = x.shape
    assert batch == 1
    depth = mix_norm.shape[0]
    x = x.reshape(seq, d)
    ffn_w_down = ffn_w_down.astype(BF16)
    conv_w_out = conv_w_out.astype(BF16)
    attn_w_o = attn_w_o.astype(BF16)
    h = None
    for i in range(depth):
        j = i // N_MIXERS
        if i % N_MIXERS == 0:
            u = _mm_glu(x, mix_norm[i], conv_w_in, j, conv_b_in[j])
            c = _conv_ln(u, conv_dw_w[j], conv_dw_b[j], conv_ln_g[j], conv_ln_b[j])
            x, h = _mm_res_norm(c, conv_w_out, j, x, ffn_norm[i], bias=conv_b_out[j])
        else:
            lambda_init = 0.8 - 0.6 * math.exp(-0.3 * i)
            n_heads = d // (2 * HEAD_DIM)
            qk = _mm_qk(h, attn_w_qkv, j, n=2 * d, n_scaled=d, scale=HEAD_DIM ** -0.5 * LOG2E)
            wt_v, ones_col = _value_weights(attn_w_qkv[j, :, 2 * d:].astype(BF16), n_heads)
            vt = _mm_nt(h, wt_v, ones_col, bm=ATTN_BLOCK, bn=wt_v.shape[0] // 2)
            bias = _bias_tiles(rel_bias, ATTN_BLOCK, seq)
            lam_vecs = jnp.stack([attn_lambda_q1[j], attn_lambda_k1[j], attn_lambda_q2[j], attn_lambda_k2[j]])
            o = _attention(qk, vt, bias, lam_vecs, attn_subln_g[j], lambda_init)
            x, h = _mm_res_norm(o, attn_w_o, j, x, ffn_norm[i])
        act = _ffn_up(h, ffn_w_up, i, ffn_dw_w[i], ffn_dw_b[i])
        last = i == depth - 1
        x, h = _mm_res_norm(act, ffn_w_down, i, x, final_norm_g if last else mix_norm[i + 1],
                            out_dtype=F32 if last else BF16, emit_x=not last)
    return h.reshape(batch, seq, d)
```

```python
import functools
import math

import numpy as np
import jax
import jax.numpy as jnp
from jax import lax
from jax.experimental import pallas as pl
from jax.experimental.pallas import tpu as pltpu

F32 = jnp.float32
BF16 = jnp.bfloat16

CHUNK = 64
HEAD_DIM = 64
NUM_BUCKETS = 32
MAX_DISTANCE = 128
EPS = 1e-6
N_MIXERS = 2

V7X_VMEM_BYTES = 64 * 1024 * 1024
V7X_F32_SUBLANES = 8
V7X_LANES = 128
V7X_MXU_WIDTH = 256
VMEM_TEMP_BYTES = 12 * 1024 * 1024

LOG2E = math.log2(math.e)

ATTN_BLOCK = 512
ATTN_UNROLL = 8
ATTN_QCHUNK = 256
V_PAD = 16
CONV_ROWS = 256
CONV_HALO = 32
FFN_CARRY = V7X_F32_SUBLANES


def _nbytes(shape, dtype):
    return int(np.prod(shape)) * jnp.dtype(dtype).itemsize


def _params(semantics, blocks, scratch=()):
    need = 2 * sum(_nbytes(s, d) for s, d in blocks) + sum(_nbytes(s, d) for s, d in scratch)
    limit = min(need + VMEM_TEMP_BYTES, V7X_VMEM_BYTES - 4 * 1024 * 1024)
    return pltpu.CompilerParams(dimension_semantics=semantics, vmem_limit_bytes=limit)


def _mm_glu_kernel(x_ref, ng_ref, wa_ref, wg_ref, ba_ref, bg_ref, o_ref, h_ref):
    @pl.when(pl.program_id(1) == 0)
    def _():
        x = x_ref[...]
        ms = jnp.mean(x * x, axis=-1, keepdims=True)
        h_ref[...] = (x * lax.rsqrt(ms + EPS) * ng_ref[...]).astype(h_ref.dtype)

    h = h_ref[...]
    a = jnp.dot(h, wa_ref[...].astype(BF16), preferred_element_type=F32) + ba_ref[...]
    g = jnp.dot(h, wg_ref[...].astype(BF16), preferred_element_type=F32) + bg_ref[...]
    o_ref[...] = a * jax.nn.sigmoid(g)


def _mm_glu(x, norm_g, w, layer, b, bm=1024, bn=512):
    s, k = x.shape
    n = w.shape[2] // 2
    nb = n // bn
    b = b.reshape(1, 2 * n)
    return pl.pallas_call(
        _mm_glu_kernel,
        grid=(s // bm, nb),
        in_specs=[
            pl.BlockSpec((bm, k), lambda i, j: (i, 0)),
            pl.BlockSpec((1, k), lambda i, j: (0, 0)),
            pl.BlockSpec((None, k, bn), lambda i, j: (layer, 0, j)),
            pl.BlockSpec((None, k, bn), lambda i, j: (layer, 0, j + nb)),
            pl.BlockSpec((1, bn), lambda i, j: (0, j)),
            pl.BlockSpec((1, bn), lambda i, j: (0, j + nb)),
        ],
        out_specs=pl.BlockSpec((bm, bn), lambda i, j: (i, j)),
        out_shape=jax.ShapeDtypeStruct((s, n), F32),
        scratch_shapes=[pltpu.VMEM((bm, k), BF16)],
        compiler_params=_params(
            ("parallel", "arbitrary"),
            [((bm, k), F32), ((k, bn), w.dtype), ((k, bn), w.dtype), ((bm, bn), F32)],
            scratch=[((bm, k), BF16)]),
        name="mm_glu",
    )(x, norm_g.reshape(1, k), w, w, b, b)


def _mm_res_norm_kernel(a_ref, w_ref, *refs, has_bias, emit_x):
    refs = list(refs)
    b_ref = refs.pop(0) if has_bias else None
    r_ref, g_ref = refs.pop(0), refs.pop(0)
    xo_ref = refs.pop(0) if emit_x else None
    ho_ref = refs.pop(0)
    y = jnp.dot(a_ref[...], w_ref[...], preferred_element_type=F32)
    if has_bias:
        y = y + b_ref[...]
    y = r_ref[...] + y
    if emit_x:
        xo_ref[...] = y
    ms = jnp.mean(y * y, axis=-1, keepdims=True)
    ho_ref[...] = (y * lax.rsqrt(ms + EPS) * g_ref[...]).astype(ho_ref.dtype)


def _mm_res_norm(a, w, layer, res, norm_g, bias=None, out_dtype=BF16, emit_x=True):
    s, k = a.shape
    n = w.shape[2]
    budget = V7X_VMEM_BYTES - 8 * 1024 * 1024 - VMEM_TEMP_BYTES
    w_bytes = _nbytes((k, n), BF16)
    w_buffers = 2 if 2 * w_bytes <= budget // 2 else 1
    row_bytes = lambda bm: (_nbytes((bm, k), BF16) + _nbytes((bm, n), F32) * (2 if emit_x else 1)
                            + _nbytes((bm, n), out_dtype))
    bm = next(c for c in (512, 256, 128) if w_buffers * w_bytes + 2 * row_bytes(c) <= budget)
    row_spec = lambda cols: pl.BlockSpec((bm, cols), lambda i: (i, 0))
    vec_spec = pl.BlockSpec((1, n), lambda i: (0, 0))
    w_mode = {} if w_buffers == 2 else {"pipeline_mode": pl.Buffered(1)}
    specs = [row_spec(k), pl.BlockSpec((None, k, n), lambda i: (layer, 0, 0), **w_mode)]
    args = [a, w]
    if bias is not None:
        specs.append(vec_spec)
        args.append(bias.reshape(1, n))
    specs += [row_spec(n), vec_spec]
    args += [res, norm_g.reshape(1, n)]
    outs = [jax.ShapeDtypeStruct((s, n), out_dtype)]
    if emit_x:
        outs.insert(0, jax.ShapeDtypeStruct((s, n), F32))
    limit = w_buffers * w_bytes + 2 * row_bytes(bm) + VMEM_TEMP_BYTES
    out = pl.pallas_call(
        functools.partial(_mm_res_norm_kernel, has_bias=bias is not None, emit_x=emit_x),
        grid=(s // bm,),
        in_specs=specs,
        out_specs=[row_spec(n)] * len(outs),
        out_shape=outs,
        compiler_params=pltpu.CompilerParams(dimension_semantics=("parallel",), vmem_limit_bytes=limit),
        name="mm_res_norm",
    )(*args)
    return (out[0], out[1]) if emit_x else (None, out[0])


def _mm_qk_body(h_ref, w_ref, sc_ref, o_ref):
    acc = jnp.dot(h_ref[...], w_ref[...].astype(BF16), preferred_element_type=F32)
    o_ref[...] = (acc * sc_ref[...]).astype(o_ref.dtype)


def _mm_qk(h, w, layer, n, n_scaled, scale, bm=1024, bn=1024):
    s, k = h.shape
    col_scale = jnp.where(jnp.arange(n) < n_scaled, scale, 1.0).astype(F32).reshape(1, n)
    w_buffers = 3

    def outer(h_hbm, w_hbm, sc_hbm, o_hbm):
        pltpu.emit_pipeline(
            _mm_qk_body,
            grid=(s // bm, n // bn),
            in_specs=[
                pl.BlockSpec((bm, k), lambda i, j: (i, 0)),
                pl.BlockSpec((k, bn), lambda i, j: (0, j), pipeline_mode=pl.Buffered(w_buffers)),
                pl.BlockSpec((1, bn), lambda i, j: (0, j)),
            ],
            out_specs=[pl.BlockSpec((bm, bn), lambda i, j: (i, j))],
        )(h_hbm, w_hbm.at[layer], sc_hbm, o_hbm)

    limit = (2 * _nbytes((bm, k), BF16) + w_buffers * _nbytes((k, bn), w.dtype) + 2 * _nbytes((bm, bn), BF16)
             + _nbytes((bm, bn), F32) + VMEM_TEMP_BYTES)
    return pl.pallas_call(
        outer,
        in_specs=[pl.BlockSpec(memory_space=pl.ANY)] * 3,
        out_specs=pl.BlockSpec(memory_space=pl.ANY),
        out_shape=jax.ShapeDtypeStruct((s, n), BF16),
        compiler_params=pltpu.CompilerParams(vmem_limit_bytes=limit),
        name="mm_qk",
    )(h, w, col_scale)


def _mm_nt_kernel(wt_ref, b_ref, h_ref, o_ref):
    acc = lax.dot_general(wt_ref[...], h_ref[...], (((1,), (1,)), ((), ())), preferred_element_type=F32)
    o_ref[0] = (acc + b_ref[...]).astype(o_ref.dtype)


def _mm_nt(h, wt, col_bias, bm, bn):
    s, k = h.shape
    n = wt.shape[0]
    return pl.pallas_call(
        _mm_nt_kernel,
        grid=(s // bm, n // bn),
        in_specs=[
            pl.BlockSpec((bn, k), lambda i, j: (j, 0)),
            pl.BlockSpec((bn, 1), lambda i, j: (j, 0)),
            pl.BlockSpec((bm, k), lambda i, j: (i, 0)),
        ],
        out_specs=pl.BlockSpec((1, bn, bm), lambda i, j: (i, j, 0)),
        out_shape=jax.ShapeDtypeStruct((s // bm, n, bm), BF16),
        compiler_params=_params(
            ("parallel", "parallel"),
            [((bn, k), BF16), ((bn, V7X_LANES), F32), ((bm, k), BF16), ((bn, bm), BF16)],
            scratch=[((bn, bm), F32)]),
        name="mm_nt",
    )(wt, col_bias, h)


def _value_weights(w_v, n_heads):
    k, d = w_v.shape
    hd = d // n_heads
    wt = jnp.pad(w_v.T.reshape(n_heads, hd, k), ((0, 0), (0, V_PAD), (0, 0)))
    ones_row = jnp.zeros((n_heads, hd + V_PAD, 1), F32).at[:, hd, :].set(1.0)
    return wt.reshape(n_heads * (hd + V_PAD), k), ones_row.reshape(n_heads * (hd + V_PAD), 1)


def _conv_ln_kernel(cur_ref, halo_ref, w_ref, b_ref, g_ref, beta_ref, o_ref, buf_ref, sh_ref, y_ref, *, taps):
    rows, d = cur_ref.shape
    n_tiles = d // V7X_LANES
    sub = V7X_F32_SUBLANES
    span = sh_ref.shape[2]
    first_block = pl.program_id(0) == 0
    for c in range(n_tiles):
        cs = slice(c * V7X_LANES, (c + 1) * V7X_LANES)
        halo = halo_ref[:, cs]
        buf_ref[c, 0:CONV_HALO, :] = jnp.where(first_block, jnp.zeros_like(halo), halo)
        buf_ref[c, CONV_HALO:CONV_HALO + rows, :] = cur_ref[:, cs]
        for s in range(1, sub):
            sh_ref[s - 1, c] = buf_ref[c, s:s + span, :]

    first = CONV_HALO - (taps - 1)

    def tile_body(c, carry):
        acc = jnp.broadcast_to(b_ref[c], (rows, V7X_LANES))
        for t in range(taps):
            base, s = divmod(first + t, sub)
            base *= sub
            src = buf_ref[c, base:base + rows, :] if s == 0 else sh_ref[s - 1, c, base:base + rows, :]
            acc = acc + w_ref[c, t:t + 1, :] * src
        y_ref[c] = acc
        return carry

    lax.fori_loop(0, n_tiles, tile_body, 0)

    total = y_ref[0]
    for c in range(1, n_tiles):
        total = total + y_ref[c]
    mu = jnp.sum(total, axis=-1, keepdims=True) * (1.0 / d)
    sq = jnp.zeros((rows, V7X_LANES), F32)
    for c in range(n_tiles):
        yc = y_ref[c] - mu
        sq = sq + yc * yc
    inv = lax.rsqrt(jnp.sum(sq, axis=-1, keepdims=True) * (1.0 / d) + EPS)
    for c in range(n_tiles):
        z = (y_ref[c] - mu) * inv * g_ref[c] + beta_ref[c]
        o_ref[:, c * V7X_LANES:(c + 1) * V7X_LANES] = (z * jax.nn.sigmoid(z)).astype(o_ref.dtype)


def _conv_ln(u, dw_w, dw_b, ln_g, ln_b):
    s, d = u.shape
    taps = dw_w.shape[0]
    rows = CONV_ROWS
    per = rows // CONV_HALO
    n_tiles = d // V7X_LANES
    vec = lambda v: v.reshape(n_tiles, 1, V7X_LANES)
    w_tiles = dw_w.reshape(taps, n_tiles, V7X_LANES).transpose(1, 0, 2)
    span = rows + ((CONV_HALO - 1) // V7X_F32_SUBLANES) * V7X_F32_SUBLANES
    buf = (n_tiles, rows + CONV_HALO, V7X_LANES)
    shifted = (V7X_F32_SUBLANES - 1, n_tiles, span, V7X_LANES)
    conv_out = (n_tiles, rows, V7X_LANES)
    vec_spec = pl.BlockSpec((n_tiles, 1, V7X_LANES), lambda i: (0, 0, 0))
    return pl.pallas_call(
        functools.partial(_conv_ln_kernel, taps=taps),
        grid=(s // rows,),
        in_specs=[
            pl.BlockSpec((rows, d), lambda i: (i, 0)),
            pl.BlockSpec((CONV_HALO, d), lambda i: (jnp.maximum(i * per - 1, 0), 0)),
            pl.BlockSpec((n_tiles, taps, V7X_LANES), lambda i: (0, 0, 0)),
            vec_spec, vec_spec, vec_spec,
        ],
        out_specs=pl.BlockSpec((rows, d), lambda i: (i, 0)),
        out_shape=jax.ShapeDtypeStruct((s, d), BF16),
        scratch_shapes=[pltpu.VMEM(buf, F32), pltpu.VMEM(shifted, F32), pltpu.VMEM(conv_out, F32)],
        compiler_params=_params(
            ("parallel",), [((rows, d), F32), ((CONV_HALO, d), F32), ((taps, d), F32), ((rows, d), BF16)],
            scratch=[(buf, F32), (shifted, F32), (conv_out, F32)]),
        name="conv_ln",
    )(u, u, w_tiles, vec(dw_b), vec(ln_g), vec(ln_b))


def _ffn_up_kernel(h_ref, wg_ref, wv_ref, cwg_ref, cwv_ref, cbg_ref, cbv_ref, o_ref,
                   wgb_ref, wvb_ref, buf_ref):
    rows = h_ref.shape[0]
    c = FFN_CARRY

    @pl.when(pl.program_id(1) == 0)
    def _():
        wgb_ref[...] = wg_ref[...].astype(BF16)
        wvb_ref[...] = wv_ref[...].astype(BF16)
        for part in range(2):
            buf_ref[part, 0:c, :] = jnp.zeros((c, buf_ref.shape[2]), F32)

    h = h_ref[...]
    buf_ref[0, c:c + rows, :] = jnp.dot(h, wgb_ref[...], preferred_element_type=F32)
    buf_ref[1, c:c + rows, :] = jnp.dot(h, wvb_ref[...], preferred_element_type=F32)

    def conv(part, cw_ref, cb_ref):
        return (cw_ref[0:1, :] * buf_ref[part, c - 2:c - 2 + rows, :]
                + cw_ref[1:2, :] * buf_ref[part, c - 1:c - 1 + rows, :]
                + cw_ref[2:3, :] * buf_ref[part, c:c + rows, :]
                + cb_ref[...])

    gate = conv(0, cwg_ref, cbg_ref)
    val = conv(1, cwv_ref, cbv_ref)
    o_ref[...] = (gate * jax.nn.sigmoid(gate) * val).astype(o_ref.dtype)
    for part in range(2):
        buf_ref[part, 0:c, :] = buf_ref[part, rows:rows + c, :]


def _ffn_up(h, w_up, layer, dw_w, dw_b, bm=1024, bn=512):
    s, k = h.shape
    f = w_up.shape[2] // 2
    nb = f // bn
    taps = dw_w.shape[0]
    assert taps - 1 <= FFN_CARRY
    dw_b = dw_b.reshape(1, 2 * f)
    buf = (2, bm + FFN_CARRY, bn)
    return pl.pallas_call(
        _ffn_up_kernel,
        grid=(nb, s // bm),
        in_specs=[
            pl.BlockSpec((bm, k), lambda j, i: (i, 0)),
            pl.BlockSpec((None, k, bn), lambda j, i: (layer, 0, j)),
            pl.BlockSpec((None, k, bn), lambda j, i: (layer, 0, j + nb)),
            pl.BlockSpec((taps, bn), lambda j, i: (0, j)),
            pl.BlockSpec((taps, bn), lambda j, i: (0, j + nb)),
            pl.BlockSpec((1, bn), lambda j, i: (0, j)),
            pl.BlockSpec((1, bn), lambda j, i: (0, j + nb)),
        ],
        out_specs=pl.BlockSpec((bm, bn), lambda j, i: (i, j)),
        out_shape=jax.ShapeDtypeStruct((s, f), BF16),
        scratch_shapes=[pltpu.VMEM((k, bn), BF16), pltpu.VMEM((k, bn), BF16), pltpu.VMEM(buf, F32)],
        compiler_params=_params(
            ("arbitrary", "arbitrary"),
            [((bm, k), BF16), ((k, bn), w_up.dtype), ((k, bn), w_up.dtype), ((bm, bn), BF16)],
            scratch=[((k, bn), BF16), ((k, bn), BF16), (buf, F32)]),
        name="ffn_up",
    )(h, w_up, w_up, dw_w, dw_w, dw_b, dw_b)


def _t5_bucket(rel):
    half = NUM_BUCKETS // 2
    max_exact = half // 2
    ret = jnp.where(rel > 0, half, 0)
    n = jnp.abs(rel)
    nf = jnp.maximum(n, 1).astype(jnp.float32)
    large = max_exact + (jnp.log(nf / max_exact) / math.log(MAX_DISTANCE / max_exact)
                         * (half - max_exact)).astype(jnp.int32)
    large = jnp.minimum(large, half - 1)
    return ret + jnp.where(n < max_exact, n, large)


def _far_bucket(block, seq):
    n = np.arange(block + 1, seq + 1, dtype=np.float64)
    half = NUM_BUCKETS // 2
    max_exact = half // 2
    large = max_exact + (np.log(n / max_exact) / math.log(MAX_DISTANCE / max_exact) * (half - max_exact)).astype(np.int64)
    assert large.min() > half - 1, "attention block too small for a constant far-key bias"
    return half - 1


def _bias_tiles_kernel(tbl_ref, bucket_ref, allowed_ref, o_ref, *, far_bucket):
    h = pl.program_id(0)
    far = tbl_ref[h, far_bucket]
    for t in range(2):
        bucket = bucket_ref[t]
        acc = jnp.zeros(bucket.shape, F32)
        for b in range(NUM_BUCKETS if t == 0 else NUM_BUCKETS // 2):
            acc = jnp.where(bucket == b, (tbl_ref[h, b] - far) * LOG2E, acc)
        o_ref[0, t] = jnp.where(allowed_ref[t] != 0, acc, -jnp.inf)


def _bias_tiles(rel_bias, block, seq):
    n_heads = rel_bias.shape[1]
    key = jnp.arange(block, dtype=jnp.int32)[:, None]
    qry = jnp.arange(block, dtype=jnp.int32)[None, :]
    rel = jnp.stack([key - qry, key - qry - block])
    bucket = _t5_bucket(rel)
    allowed = jnp.stack([(key // CHUNK) <= (qry // CHUNK), jnp.ones((block, block), bool)]).astype(jnp.int32)
    return pl.pallas_call(
        functools.partial(_bias_tiles_kernel, far_bucket=_far_bucket(block, seq)),
        grid=(n_heads,),
        in_specs=[
            pl.BlockSpec(memory_space=pltpu.SMEM),
            pl.BlockSpec((2, block, block), lambda h: (0, 0, 0)),
            pl.BlockSpec((2, block, block), lambda h: (0, 0, 0)),
        ],
        out_specs=pl.BlockSpec((1, 2, block, block), lambda h: (h, 0, 0, 0)),
        out_shape=jax.ShapeDtypeStruct((n_heads, 2, block, block), F32),
        compiler_params=_params(
            ("parallel",),
            [((2, block, block), jnp.int32), ((2, block, block), jnp.int32), ((2, block, block), F32)]),
        name="bias_tiles",
    )(rel_bias.T, bucket, allowed)


def _attn_kernel(q_ref, qn_ref, k_ref, vt_ref, bias_ref, lam_ref, g_ref, o_ref, qp_ref, s_ref, m_ref, acc_ref, *,
                 lambda_init):
    blk, hd = q_ref.shape
    qi = pl.program_id(1)
    q = q_ref[...]
    lane = lax.broadcasted_iota(jnp.int32, q.shape, 1)
    zero = jnp.zeros_like(q)
    qp_ref[0] = jnp.where(lane < HEAD_DIM, q, zero)
    qp_ref[1] = jnp.where(lane >= HEAD_DIM, q, zero)
    m_ref[...] = jnp.full(m_ref.shape, -jnp.inf, F32)
    acc_ref[...] = jnp.zeros(acc_ref.shape, F32)

    def scores(c, kb):
        k = k_ref[pl.ds(pl.multiple_of(kb * blk, blk), blk), :]
        s_ref[c] = lax.dot_general(k, qp_ref[c], (((1,), (1,)), ((), ())), preferred_element_type=F32)

    def accumulate(c, kb, bias_tile):
        vt = vt_ref[kb]
        for j in range(blk // ATTN_QCHUNK):
            cs = pl.ds(j * ATTN_QCHUNK, ATTN_QCHUNK)
            s = s_ref[c, :, cs]
            if bias_tile is not None:
                s = s + bias_ref[0, bias_tile, :, cs]
            m_old = m_ref[c, :, cs]
            m_new = jnp.maximum(m_old, jnp.max(s, axis=0, keepdims=True))
            alpha = jnp.exp2(m_old - m_new)
            p = jnp.exp2(s - m_new).astype(vt.dtype)
            acc_ref[c, :, cs] = alpha * acc_ref[c, :, cs] + jnp.dot(vt, p, preferred_element_type=F32)
            m_ref[c, :, cs] = m_new

    def block_step(kb, bias_tile, has_next):
        scores(1, kb)
        accumulate(0, kb, bias_tile)
        if has_next:
            scores(0, kb + 1)
        accumulate(1, kb, bias_tile)

    @pl.when(qi == 0)
    def _():
        scores(0, 0)

    def far_blocks(first, count):
        for u in range(count):
            block_step(first + u, None, True)

    def far_group(group, carry):
        far_blocks(ATTN_UNROLL * group, ATTN_UNROLL)
        return carry

    n_far = jnp.maximum(qi - 1, 0)
    n_grouped = (n_far // ATTN_UNROLL) * ATTN_UNROLL
    lax.fori_loop(0, n_far // ATTN_UNROLL, far_group, 0)
    rest = n_far - n_grouped
    size = ATTN_UNROLL // 2
    while size >= 1:
        first = n_grouped + (rest // (2 * size)) * (2 * size)
        pl.when((rest // size) % 2 == 1)(functools.partial(far_blocks, first, size))
        size //= 2

    @pl.when(qi >= 1)
    def _():
        block_step(qi - 1, 1, True)
        block_step(qi, 0, False)

    @pl.when(qi == 0)
    def _():
        block_step(qi, 0, False)

    q_next = qn_ref[...]
    s_ref[0] = lax.dot_general(k_ref[0:blk, :], jnp.where(lane < HEAD_DIM, q_next, jnp.zeros_like(q_next)),
                               (((1,), (1,)), ((), ())), preferred_element_type=F32)

    lam_v = lam_ref[...]
    lam = (jnp.exp(jnp.sum(lam_v[0:1] * lam_v[1:2], axis=-1, keepdims=True))
           - jnp.exp(jnp.sum(lam_v[2:3] * lam_v[3:4], axis=-1, keepdims=True)) + lambda_init)
    o = (acc_ref[0, 0:hd, :] / acc_ref[0, hd:hd + 1, :]
         - lam * (acc_ref[1, 0:hd, :] / acc_ref[1, hd:hd + 1, :]))
    ms = jnp.mean(o * o, axis=0, keepdims=True)
    y = o * lax.rsqrt(ms + EPS) * g_ref[...] * (1.0 - lambda_init)
    o_ref[...] = y.T.astype(o_ref.dtype)


def _attention(qk, vt, bias, lam_vecs, subln_g, lambda_init):
    s, d2 = qk.shape
    d = d2 // 2
    nb, _, blk = vt.shape
    hd = 2 * HEAD_DIM
    hv = hd + V_PAD
    n_heads = d // hd
    return pl.pallas_call(
        functools.partial(_attn_kernel, lambda_init=lambda_init),
        grid=(n_heads, nb),
        in_specs=[
            pl.BlockSpec((blk, hd), lambda h, i: (i, h)),
            pl.BlockSpec((blk, hd), lambda h, i: (jnp.minimum(i + 1, nb - 1), h)),
            pl.BlockSpec((s, hd), lambda h, i: (0, n_heads + h)),
            pl.BlockSpec((nb, hv, blk), lambda h, i: (0, h, 0)),
            pl.BlockSpec((1, 2, blk, blk), lambda h, i: (h, 0, 0, 0)),
            pl.BlockSpec((4, HEAD_DIM), lambda h, i: (0, 0)),
            pl.BlockSpec((hd, 1), lambda h, i: (0, 0)),
        ],
        out_specs=pl.BlockSpec((blk, hd), lambda h, i: (i, h)),
        out_shape=jax.ShapeDtypeStruct((s, d), BF16),
        scratch_shapes=[
            pltpu.VMEM((2, blk, hd), BF16), pltpu.VMEM((2, blk, blk), F32),
            pltpu.VMEM((2, 1, blk), F32), pltpu.VMEM((2, hv, blk), F32)],
        compiler_params=_params(
            ("arbitrary", "arbitrary"),
            [((blk, hd), BF16), ((blk, hd), BF16), ((s, hd), BF16), ((nb, hv, blk), BF16), ((2, blk, blk), F32),
             ((blk, hd), BF16)],
            scratch=[((2, blk, hd), BF16), ((2, blk, blk), F32), ((2, hv, blk), F32)]),
        name="diff_attention",
    )(qk, qk, qk, vt, bias, lam_vecs, subln_g.reshape(hd, 1))


def kernel(x, mix_norm, ffn_norm, conv_w_in, conv_b_in, conv_dw_w, conv_dw_b, conv_ln_g, conv_ln_b, conv_w_out, conv_b_out, attn_w_qkv, attn_lambda_q1, attn_lambda_k1, attn_lambda_q2, attn_lambda_k2, attn_subln_g, attn_w_o, rel_bias, ffn_w_up, ffn_dw_w, ffn_dw_b, ffn_w_down, final_norm_g):
    batch, seq, d = x.shape
    assert batch == 1
    depth = mix_norm.shape[0]
    x = x.reshape(seq, d)
    ffn_w_down = ffn_w_down.astype(BF16)
    conv_w_out = conv_w_out.astype(BF16)
    attn_w_o = attn_w_o.astype(BF16)
    h = None
    for i in range(depth):
        j = i // N_MIXERS
        if i % N_MIXERS == 0:
            u = _mm_glu(x, mix_norm[i], conv_w_in, j, conv_b_in[j])
            c = _conv_ln(u, conv_dw_w[j], conv_dw_b[j], conv_ln_g[j], conv_ln_b[j])
            x, h = _mm_res_norm(c, conv_w_out, j, x, ffn_norm[i], bias=conv_b_out[j])
        else:
            lambda_init = 0.8 - 0.6 * math.exp(-0.3 * i)
            n_heads = d // (2 * HEAD_DIM)
            qk = _mm_qk(h, attn_w_qkv, j, n=2 * d, n_scaled=d, scale=HEAD_DIM ** -0.5 * LOG2E)
            wt_v, ones_col = _value_weights(attn_w_qkv[j, :, 2 * d:].astype(BF16), n_heads)
            vt = _mm_nt(h, wt_v, ones_col, bm=ATTN_BLOCK, bn=wt_v.shape[0] // 2)
            bias = _bias_tiles(rel_bias, ATTN_BLOCK, seq)
            lam_vecs = jnp.stack([attn_lambda_q1[j], attn_lambda_k1[j], attn_lambda_q2[j], attn_lambda_k2[j]])
            o = _attention(qk, vt, bias, lam_vecs, attn_subln_g[j], lambda_init)
            x, h = _mm_res_norm(o, attn_w_o, j, x, ffn_norm[i])
        act = _ffn_up(h, ffn_w_up, i, ffn_dw_w[i], ffn_dw_b[i])
        last = i == depth - 1
        x, h = _mm_res_norm(act, ffn_w_down, i, x, final_norm_g if last else mix_norm[i + 1],
                            out_dtype=F32 if last else BF16, emit_x=not last)
    return h.reshape(batch, seq, d)
```
